```python
import functools
import jax, jax.numpy as jnp
from jax import lax
import numpy as np

D_MODEL = 1024
BATCH = 2
SEQ = 16384
DEPTH = 1
DEC_BATCH = 128
DEC_SEQ = 4
PAST_LEN = 8192
PAGE_SIZE = 128

MIX_WIDTH = D_MODEL
ATTN_WIDTH = MIX_WIDTH // 2
RNN_WIDTH = MIX_WIDTH - ATTN_WIDTH
HEAD_DIM = 64
N_HEADS = ATTN_WIDTH // HEAD_DIM
N_KV_HEADS = N_HEADS
KV_WIDTH = N_KV_HEADS * HEAD_DIM
LRU_BLOCKS = 8
LRU_BLOCK_W = RNN_WIDTH // LRU_BLOCKS
CONV_WIDTH = 4
LRU_C = 8.0
BLOCK = 256
TOP_BLOCKS = 3
Q_CHUNK = 128
N_GROUPS = 4
EXPERTS_PER_GROUP = 8
N_EXPERTS = N_GROUPS * EXPERTS_PER_GROUP
TOP_E = 2
D_EXPERT = D_MODEL // 2
MOE_BLOCK = 128
EPS = 1e-6
D_IN = ATTN_WIDTH + 2 * KV_WIDTH + 2 * RNN_WIDTH

kernel_name = "hymba_moba_rglru_hmoe_step"


def rmsnorm(x, g):
    x32 = x.astype(jnp.float32)
    y = x32 * lax.rsqrt(jnp.mean(x32 * x32, axis=-1, keepdims=True) + EPS) * g.astype(jnp.float32)
    return y.astype(x.dtype)


def alibi_slopes():
    return 2.0 ** (-8.0 * jnp.arange(1, N_HEADS + 1, dtype=jnp.float32) / N_HEADS)


def moba_prompt(q, k, v, slopes):
    B, S, H, Dh = q.shape
    n_full = S // BLOCK
    k_top = min(TOP_BLOCKS, n_full)
    local_len = min(BLOCK, S)
    scale = HEAD_DIM ** -0.5
    if k_top > 0:
        kb = k[:, :n_full * BLOCK].reshape(B, n_full, BLOCK, H, Dh)
        vb = v[:, :n_full * BLOCK].reshape(B, n_full, BLOCK, H, Dh)
        kmean = jnp.mean(kb, axis=2, dtype=jnp.float32)
    bi = jnp.arange(B)[:, None, None, None]
    hi = jnp.arange(H)[None, None, :, None]

    def chunk(c):
        t0 = c * Q_CHUNK
        qc = lax.dynamic_slice_in_dim(q, t0, Q_CHUNK, axis=1)
        tq = t0 + jnp.arange(Q_CHUNK)
        own_start = (t0 // BLOCK) * BLOCK
        l0 = jnp.clip(t0 + Q_CHUNK - local_len, 0, S - local_len)
        kl = lax.dynamic_slice_in_dim(k, l0, local_len, axis=1)
        vl = lax.dynamic_slice_in_dim(v, l0, local_len, axis=1)
        sl = l0 + jnp.arange(local_len)
        dist_l = (tq[:, None] - sl[None, :]).astype(jnp.float32)
        s_loc = (jnp.einsum('bqhd,blhd->bqhl', qc, kl, preferred_element_type=jnp.float32) * scale
                 - slopes[None, :, None] * dist_l[:, None, :])
        mask_l = (sl[None, :] <= tq[:, None]) & (sl[None, :] >= own_start)
        s_loc = jnp.where(mask_l[:, None, :], s_loc, -jnp.inf)
        if k_top > 0:
            qb = t0 // BLOCK
            gate = jnp.einsum('bqhd,bnhd->bqhn', qc, kmean, preferred_element_type=jnp.float32)
            gate = jnp.where(jnp.arange(n_full) < qb, gate, -jnp.inf)
            _, idx = lax.top_k(gate, k_top)
            valid = idx < qb
            ks = kb[bi, idx, :, hi, :]
            vs = vb[bi, idx, :, hi, :]
            sk = idx[..., None] * BLOCK + jnp.arange(BLOCK)
            dist_s = (tq[None, :, None, None, None] - sk).astype(jnp.float32)
            s_sel = (jnp.einsum('bqhd,bqhnjd->bqhnj', qc, ks, preferred_element_type=jnp.float32) * scale
                     - slopes[None, None, :, None, None] * dist_s)
            s_sel = jnp.where(valid[..., None], s_sel, -jnp.inf).reshape(B, Q_CHUNK, H, k_top * BLOCK)
            p = jax.nn.softmax(jnp.concatenate([s_sel, s_loc], axis=-1), axis=-1)
            p_sel = p[..., :k_top * BLOCK].reshape(B, Q_CHUNK, H, k_top, BLOCK)
            p_loc = p[..., k_top * BLOCK:]
            return (jnp.einsum('bqhl,blhd->bqhd', p_loc, vl)
                    + jnp.einsum('bqhnj,bqhnjd->bqhd', p_sel, vs))
        p_loc = jax.nn.softmax(s_loc, axis=-1)
        return jnp.einsum('bqhl,blhd->bqhd', p_loc, vl)

    outs = lax.map(chunk, jnp.arange(S // Q_CHUNK))
    return outs.transpose(1, 0, 2, 3, 4).reshape(B, S, H, Dh)


def moba_sample(q, k, v, slopes, cache_k, cache_v, page_table):
    DB, T, H, Dh = q.shape
    PS = cache_k.shape[1]
    past = page_table.shape[1] * PS
    ppb = BLOCK // PS
    qb = past // BLOCK
    k_top = min(TOP_BLOCKS, qb)
    own_start = qb * BLOCK
    n_own = (past - own_start) // PS
    scale = HEAD_DIM ** -0.5
    tq = past + jnp.arange(T)
    own_pages = page_table[:, qb * ppb: qb * ppb + n_own]
    kl = jnp.concatenate([cache_k[own_pages].reshape(DB, n_own * PS, H, Dh).astype(k.dtype), k], axis=1)
    vl = jnp.concatenate([cache_v[own_pages].reshape(DB, n_own * PS, H, Dh).astype(v.dtype), v], axis=1)
    sl = jnp.concatenate([own_start + jnp.arange(n_own * PS), tq])
    dist_l = (tq[:, None] - sl[None, :]).astype(jnp.float32)
    s_loc = (jnp.einsum('bqhd,blhd->bqhl', q, kl, preferred_element_type=jnp.float32) * scale
             - slopes[None, :, None] * dist_l[:, None, :])
    s_loc = jnp.where((sl[None, :] <= tq[:, None])[:, None, :], s_loc, -jnp.inf)
    if k_top > 0:
        full_pages = page_table[:, :qb * ppb]
        kmean = jnp.mean(cache_k[full_pages].reshape(DB, qb, BLOCK, H, Dh), axis=2, dtype=jnp.float32)
        gate = jnp.einsum('bqhd,bnhd->bqhn', q, kmean, preferred_element_type=jnp.float32)
        _, idx = lax.top_k(gate, k_top)
        bi = jnp.arange(DB)[:, None, None, None, None]
        phys = page_table[bi, idx[..., None] * ppb + jnp.arange(ppb)]
        hi = jnp.arange(H)[None, None, :, None, None, None]
        rows = jnp.arange(PS)
        ks = cache_k[phys[..., None], rows, hi].reshape(DB, T, H, k_top, BLOCK, Dh)
        vs = cache_v[phys[..., None], rows, hi].reshape(DB, T, H, k_top, BLOCK, Dh)
        sk = idx[..., None] * BLOCK + jnp.arange(BLOCK)
        dist_s = (tq[None, :, None, None, None] - sk).astype(jnp.float32)
        s_sel = (jnp.einsum('bqhd,bqhnjd->bqhnj', q, ks, preferred_element_type=jnp.float32) * scale
                 - slopes[None, None, :, None, None] * dist_s).reshape(DB, T, H, k_top * BLOCK)
        p = jax.nn.softmax(jnp.concatenate([s_sel, s_loc], axis=-1), axis=-1)
        p_sel = p[..., :k_top * BLOCK].reshape(DB, T, H, k_top, BLOCK)
        p_loc = p[..., k_top * BLOCK:]
        return (jnp.einsum('bqhl,blhd->bqhd', p_loc, vl)
                + jnp.einsum('bqhnj,bqhnjd->bqhd', p_sel, vs))
    p_loc = jax.nn.softmax(s_loc, axis=-1)
    return jnp.einsum('bqhl,blhd->bqhd', p_loc, vl)


def causal_conv(u, buf, w, b):
    T = u.shape[1]
    padded = jnp.concatenate([buf.astype(u.dtype), u], axis=1)
    out = b + sum(w[j] * padded[:, j:j + T] for j in range(CONV_WIDTH))
    return out, padded[:, T:]


def _lin_combine(c1, c2):
    a1, b1 = c1
    a2, b2 = c2
    return a1 * a2, a2 * b1 + b2


def rglru(xc, h0, wa, ba, wx, bx, lam):
    B, T, C = xc.shape
    xg = xc.reshape(B, T, LRU_BLOCKS, LRU_BLOCK_W)
    r = jax.nn.sigmoid((jnp.einsum('btnc,ncd->btnd', xg, wa).reshape(B, T, C) + ba).astype(jnp.float32))
    i = jax.nn.sigmoid((jnp.einsum('btnc,ncd->btnd', xg, wx).reshape(B, T, C) + bx).astype(jnp.float32))
    log_a = -LRU_C * r * jax.nn.softplus(-lam.astype(jnp.float32))
    a = jnp.exp(log_a)
    bterm = jnp.sqrt(-jnp.expm1(2.0 * log_a)) * i * xc.astype(jnp.float32)
    bterm = bterm.at[:, 0].add(a[:, 0] * h0.astype(jnp.float32))
    _, h = lax.associative_scan(_lin_combine, (a, bterm), axis=1)
    return h, h[:, -1]


def hier_route(xn, rgw, rgb, rew, reb):
    N = xn.shape[0]
    gl = (xn @ rgw).astype(jnp.float32) + rgb
    pg = jax.nn.softmax(gl, axis=-1)
    gsel = jnp.argmax(gl, axis=-1)
    p_sel = jnp.take_along_axis(pg, gsel[:, None], axis=1)
    el = ((xn @ rew).astype(jnp.float32) + reb).reshape(N, N_GROUPS, EXPERTS_PER_GROUP)
    el_sel = jnp.take_along_axis(el, gsel[:, None, None], axis=1)[:, 0]
    pe = jax.nn.softmax(el_sel, axis=-1)
    w2, e2 = lax.top_k(pe, TOP_E)
    w2 = w2 / jnp.sum(w2, axis=-1, keepdims=True)
    return gsel[:, None] * EXPERTS_PER_GROUP + e2, p_sel * w2


def moe_ffn(xn, expert_ids, expert_wts, w_gate, w_up, w_down):
    N, D = xn.shape
    E = w_gate.shape[0]
    K = expert_ids.shape[1]
    A = N * K
    flat_e = expert_ids.reshape(A)
    flat_tok = jnp.repeat(jnp.arange(N, dtype=jnp.int32), K)
    flat_w = expert_wts.reshape(A)
    order = jnp.argsort(flat_e)
    se, st, sw = flat_e[order], flat_tok[order], flat_w[order]
    counts = jnp.bincount(flat_e, length=E)
    padded = (counts + MOE_BLOCK - 1) // MOE_BLOCK * MOE_BLOCK
    pad_end = jnp.cumsum(padded)
    pad_start = pad_end - padded
    start = jnp.cumsum(counts) - counts
    dest = pad_start[se] + jnp.arange(A) - start[se]
    nb = -(-A // MOE_BLOCK) + E
    R = nb * MOE_BLOCK
    buf_tok = jnp.full((R,), N, jnp.int32).at[dest].set(st)
    buf_w = jnp.zeros((R,), flat_w.dtype).at[dest].set(sw)
    block_expert = jnp.clip(jnp.searchsorted(pad_end, jnp.arange(nb) * MOE_BLOCK, side='right'), 0, E - 1)
    x_pad = jnp.concatenate([xn, jnp.zeros((1, D), xn.dtype)], axis=0)
    xb = x_pad[buf_tok].reshape(nb, MOE_BLOCK, D)

    def expert_block(args):
        xblk, e = args
        hdn = jax.nn.silu(xblk @ w_gate[e]) * (xblk @ w_up[e])
        return hdn @ w_down[e]

    yb = lax.map(expert_block, (xb, block_expert)).reshape(R, D)
    y = jax.ops.segment_sum(yb * buf_w[:, None], buf_tok, num_segments=N + 1)[:N]
    return y.astype(xn.dtype)


def trunk_layer(x, attn_core, conv_buf, h0, norm_mix, w_in, q_norm, k_norm, conv_w, conv_b,
                lru_wa, lru_ba, lru_wx, lru_bx, lru_lambda, attn_out_norm, rnn_out_norm, w_out,
                norm_ffn, rgw, rgb, rew, reb, egate, eup, edown):
    B, T, D = x.shape
    xn = rmsnorm(x, norm_mix)
    z = xn @ w_in
    q, k, v, u, g = jnp.split(z, [ATTN_WIDTH, ATTN_WIDTH + KV_WIDTH, ATTN_WIDTH + 2 * KV_WIDTH,
                                  ATTN_WIDTH + 2 * KV_WIDTH + RNN_WIDTH], axis=-1)
    q = rmsnorm(q.reshape(B, T, N_HEADS, HEAD_DIM), q_norm)
    k = rmsnorm(k.reshape(B, T, N_KV_HEADS, HEAD_DIM), k_norm)
    v = v.reshape(B, T, N_KV_HEADS, HEAD_DIM)
    o_attn = attn_core(q, k, v).reshape(B, T, ATTN_WIDTH).astype(x.dtype)
    xc, new_buf = causal_conv(u, conv_buf, conv_w, conv_b)
    h, h_last = rglru(xc, h0, lru_wa, lru_ba, lru_wx, lru_bx, lru_lambda)
    o_rnn = (h * jax.nn.gelu(g.astype(jnp.float32))).astype(x.dtype)
    mixed = jnp.concatenate([rmsnorm(o_attn, attn_out_norm), rmsnorm(o_rnn, rnn_out_norm)], axis=-1)
    x = x + mixed @ w_out
    xn2 = rmsnorm(x, norm_ffn).reshape(B * T, D)
    ids, wts = hier_route(xn2, rgw, rgb, rew, reb)
    x = x + moe_ffn(xn2, ids, wts, egate, eup, edown).reshape(B, T, D)
    return x, k, v, new_buf, h_last


def setup_inputs(seed: int = 0) -> dict:
    key = jax.random.key(seed)
    ks = jax.random.split(key, 32)
    f32 = jnp.float32

    def nrm(k, shape, s):
        return s * jax.random.normal(k, shape, f32)

    n_pages = PAST_LEN // PAGE_SIZE
    used = DEC_BATCH * n_pages
    n_pool = used + max(1, used // 4)
    a0 = jax.random.uniform(ks[17], (DEPTH, RNN_WIDTH), f32, 0.9, 0.999)
    return {
        "x_prompt": nrm(ks[0], (BATCH, SEQ, D_MODEL), 1.0),
        "x_sample": nrm(ks[1], (DEC_BATCH, DEC_SEQ, D_MODEL), 1.0),
        "cache_k": nrm(ks[2], (DEPTH, n_pool, PAGE_SIZE, N_KV_HEADS, HEAD_DIM), 1.0),
        "cache_v": nrm(ks[3], (DEPTH, n_pool, PAGE_SIZE, N_KV_HEADS, HEAD_DIM), 1.0),
        "state_conv": nrm(ks[4], (DEPTH, DEC_BATCH, CONV_WIDTH - 1, RNN_WIDTH), 1.0),
        "state_h": nrm(ks[5], (DEPTH, DEC_BATCH, RNN_WIDTH), 0.5),
        "page_table": jax.random.permutation(ks[6], n_pool)[:used].reshape(DEC_BATCH, n_pages).astype(jnp.int32),
        "norm_mix": 1.0 + nrm(ks[7], (DEPTH, D_MODEL), 0.01),
        "w_in": nrm(ks[8], (DEPTH, D_MODEL, D_IN), D_MODEL ** -0.5),
        "q_norm": 1.0 + nrm(ks[9], (DEPTH, HEAD_DIM), 0.01),
        "k_norm": 1.0 + nrm(ks[10], (DEPTH, HEAD_DIM), 0.01),
        "conv_w": nrm(ks[11], (DEPTH, CONV_WIDTH, RNN_WIDTH), CONV_WIDTH ** -0.5),
        "conv_b": nrm(ks[12], (DEPTH, RNN_WIDTH), 0.01),
        "lru_wa": nrm(ks[13], (DEPTH, LRU_BLOCKS, LRU_BLOCK_W, LRU_BLOCK_W), LRU_BLOCK_W ** -0.5),
        "lru_ba": nrm(ks[14], (DEPTH, RNN_WIDTH), 0.01),
        "lru_wx": nrm(ks[15], (DEPTH, LRU_BLOCKS, LRU_BLOCK_W, LRU_BLOCK_W), LRU_BLOCK_W ** -0.5),
        "lru_bx": nrm(ks[16], (DEPTH, RNN_WIDTH), 0.01),
        "lru_lambda": jnp.log(a0) - jnp.log1p(-a0),
        "attn_out_norm": 1.0 + nrm(ks[18], (DEPTH, ATTN_WIDTH), 0.01),
        "rnn_out_norm": 1.0 + nrm(ks[19], (DEPTH, RNN_WIDTH), 0.01),
        "w_out": nrm(ks[20], (DEPTH, MIX_WIDTH, D_MODEL), MIX_WIDTH ** -0.5),
        "norm_ffn": 1.0 + nrm(ks[21], (DEPTH, D_MODEL), 0.01),
        "router_group_w": nrm(ks[22], (DEPTH, D_MODEL, N_GROUPS), D_MODEL ** -0.5),
        "router_group_b": nrm(ks[23], (DEPTH, N_GROUPS), 0.01),
        "router_expert_w": nrm(ks[24], (DEPTH, D_MODEL, N_EXPERTS), D_MODEL ** -0.5),
        "router_expert_b": nrm(ks[25], (DEPTH, N_EXPERTS), 0.01),
        "expert_w_gate": nrm(ks[26], (DEPTH, N_EXPERTS, D_MODEL, D_EXPERT), D_MODEL ** -0.5),
        "expert_w_up": nrm(ks[27], (DEPTH, N_EXPERTS, D_MODEL, D_EXPERT), D_MODEL ** -0.5),
        "expert_w_down": nrm(ks[28], (DEPTH, N_EXPERTS, D_EXPERT, D_MODEL), D_EXPERT ** -0.5),
    }


def reference(x_prompt, x_sample, cache_k, cache_v, state_conv, state_h, page_table,
              norm_mix, w_in, q_norm, k_norm, conv_w, conv_b, lru_wa, lru_ba, lru_wx, lru_bx,
              lru_lambda, attn_out_norm, rnn_out_norm, w_out, norm_ffn, router_group_w,
              router_group_b, router_expert_w, router_expert_b, expert_w_gate, expert_w_up,
              expert_w_down):
    slopes = alibi_slopes()
    yp, ys = x_prompt, x_sample
    kp_l, vp_l, cp_l, hp_l, ks_l, vs_l, cs_l, hs_l = [], [], [], [], [], [], [], []
    for l in range(DEPTH):
        lp = (norm_mix[l], w_in[l], q_norm[l], k_norm[l], conv_w[l], conv_b[l], lru_wa[l], lru_ba[l],
              lru_wx[l], lru_bx[l], lru_lambda[l], attn_out_norm[l], rnn_out_norm[l], w_out[l],
              norm_ffn[l], router_group_w[l], router_group_b[l], router_expert_w[l],
              router_expert_b[l], expert_w_gate[l], expert_w_up[l], expert_w_down[l])
        prompt_core = functools.partial(moba_prompt, slopes=slopes)
        sample_core = functools.partial(moba_sample, slopes=slopes, cache_k=cache_k[l],
                                        cache_v=cache_v[l], page_table=page_table)
        buf0 = jnp.zeros((yp.shape[0], CONV_WIDTH - 1, RNN_WIDTH), yp.dtype)
        h00 = jnp.zeros((yp.shape[0], RNN_WIDTH), jnp.float32)
        yp, kp, vp, cp, hp = trunk_layer(yp, prompt_core, buf0, h00, *lp)
        ys, ks_, vs_, cs, hs = trunk_layer(ys, sample_core, state_conv[l], state_h[l], *lp)
        kp_l.append(kp); vp_l.append(vp); cp_l.append(cp); hp_l.append(hp)
        ks_l.append(ks_); vs_l.append(vs_); cs_l.append(cs); hs_l.append(hs)
    k_prompt = jnp.stack(kp_l)
    v_prompt = jnp.stack(vp_l)
    conv_prompt = jnp.stack(cp_l)
    h_prompt = jnp.stack(hp_l)
    k_sample = jnp.stack(ks_l)
    v_sample = jnp.stack(vs_l)
    conv_sample = jnp.stack(cs_l)
    h_sample = jnp.stack(hs_l)
    return (yp, ys, k_prompt, v_prompt, conv_prompt, h_prompt, k_sample, v_sample, conv_sample, h_sample)
```

```python
import functools

import jax
import jax.numpy as jnp
import numpy as np
from jax import lax
from jax.experimental import pallas as pl
from jax.experimental.pallas import tpu as pltpu

F32 = jnp.float32
BF16 = jnp.bfloat16

D_MODEL = 1024
ATTN_WIDTH = 512
RNN_WIDTH = 512
HEAD_DIM = 64
N_HEADS = 8
LRU_BLOCKS = 8
CONV_WIDTH = 4
LRU_C = 8.0
BLOCK = 256
TOP_BLOCKS = 3
N_GROUPS = 4
EXPERTS_PER_GROUP = 8
N_EXPERTS = 32
D_EXPERT = 512
EPS = 1e-6
D_IN = ATTN_WIDTH + 2 * ATTN_WIDTH + 2 * RNN_WIDTH
NEG = -1e30
LOG2E = 1.4426950408889634

VMEM_LIMIT = 56 * 1024 * 1024


def _cparams(sem):
    return pltpu.CompilerParams(dimension_semantics=sem, vmem_limit_bytes=VMEM_LIMIT)


def _split_bf16(a):
    hi = a.astype(BF16)
    lo = (a - hi.astype(F32)).astype(BF16)
    return hi, lo


def _inproj_kernel(x_ref, nw_ref, w_ref, qn_ref, kn_ref, gm_ref,
                   q_ref, k_ref, v_ref, kb_ref, vb_ref, u_ref, g_ref, km_ref):
    x = x_ref[...]
    ms = jnp.mean(x * x, axis=-1, keepdims=True)
    xn = x * lax.rsqrt(ms + EPS) * nw_ref[...]
    z = jnp.dot(xn.astype(BF16), w_ref[...], preferred_element_type=F32)
    aw = ATTN_WIDTH

    def head_norm(t, gain):
        hi, lo = _split_bf16(t * t)
        hm = (jnp.dot(hi, gm_ref[...], preferred_element_type=F32)
              + jnp.dot(lo, gm_ref[...], preferred_element_type=F32))
        return t * lax.rsqrt(hm + EPS) * gain

    q = head_norm(z[:, 0:aw], qn_ref[...])
    k = head_norm(z[:, aw:2 * aw], kn_ref[...])
    v = z[:, 2 * aw:3 * aw]
    q_ref[...] = q
    k_ref[...] = k
    v_ref[...] = v
    kb_ref[...] = k.astype(BF16)
    vb_ref[...] = v.astype(BF16)
    u_ref[...] = z[:, 3 * aw:3 * aw + RNN_WIDTH]
    g_ref[...] = z[:, 3 * aw + RNN_WIDTH:]
    tm = k.shape[0]
    km_ref[0] = jnp.mean(k.reshape(tm // BLOCK, BLOCK, aw), axis=1)


def _head_avg_matrix():
    r = jnp.arange(ATTN_WIDTH) // HEAD_DIM
    return jnp.where(r[:, None] == r[None, :], 1.0 / HEAD_DIM, 0.0).astype(BF16)


def in_proj(x2d, norm_w, w_in_bf, q_norm, k_norm, tm):
    n = x2d.shape[0]
    assert n % tm == 0 and tm % BLOCK == 0
    nt = n // tm
    row = lambda i: (i, 0)
    const = lambda i: (0, 0)
    wspec = lambda shp: pl.BlockSpec(shp, const)
    out_f = jax.ShapeDtypeStruct((n, ATTN_WIDTH), F32)
    out_b = jax.ShapeDtypeStruct((n, ATTN_WIDTH), BF16)
    ospec = pl.BlockSpec((tm, ATTN_WIDTH), row)
    return pl.pallas_call(
        _inproj_kernel,
        grid=(nt,),
        in_specs=[pl.BlockSpec((tm, D_MODEL), row), wspec((1, D_MODEL)), wspec((D_MODEL, D_IN)),
                  wspec((1, ATTN_WIDTH)), wspec((1, ATTN_WIDTH)), wspec((ATTN_WIDTH, ATTN_WIDTH))],
        out_specs=[ospec, ospec, ospec, ospec, ospec, ospec, ospec,
                   pl.BlockSpec((1, tm // BLOCK, ATTN_WIDTH), lambda i: (i, 0, 0))],
        out_shape=[out_f, out_f, out_f, out_b, out_b, out_f, out_f,
                   jax.ShapeDtypeStruct((nt, tm // BLOCK, ATTN_WIDTH), F32)],
        compiler_params=_cparams(("arbitrary",)),
        name="in_proj",
    )(x2d, norm_w.reshape(1, D_MODEL), w_in_bf,
      jnp.tile(q_norm, N_HEADS).reshape(1, ATTN_WIDTH), jnp.tile(k_norm, N_HEADS).reshape(1, ATTN_WIDTH),
      _head_avg_matrix())


def _gelu_tanh(x):
    return 0.5 * x * (1.0 + jnp.tanh(0.7978845608028654 * (x + 0.044715 * (x * x * x))))


def _softplus(z):
    return jnp.maximum(z, 0.0) + jnp.log1p(jnp.exp(-jnp.abs(z)))


def _lru_coeffs(xc, wa_ref, ba_ref, wx_ref, bx_ref, lam_ref):
    xb = xc.astype(BF16)
    r = jax.nn.sigmoid(jnp.dot(xb, wa_ref[...], preferred_element_type=F32) + ba_ref[...])
    i = jax.nn.sigmoid(jnp.dot(xb, wx_ref[...], preferred_element_type=F32) + bx_ref[...])
    log_a = -LRU_C * r * _softplus(-lam_ref[...])
    a = jnp.exp(log_a)
    b = jnp.sqrt(-jnp.tanh(log_a) * (a * a + 1.0)) * i * xc
    return a, b


def _rms_rows(x, gain):
    return x * lax.rsqrt(jnp.mean(x * x, axis=-1, keepdims=True) + EPS) * gain


def _rnn_seq_kernel(u_ref, g_ref, cb_ref, h0_ref, cw_ref, cbias_ref, wa_ref, ba_ref, wx_ref, bx_ref,
                    lam_ref, onw_ref, o_ref, hl_ref, cl_ref, pad_ref, h_ref):
    s = pl.program_id(1)
    ts = u_ref.shape[1]
    w = u_ref.shape[2]

    @pl.when(s == 0)
    def _():
        pad_ref[0:8, :] = cb_ref[0]
        h_ref[...] = h0_ref[0]

    u = u_ref[0]
    pad_ref[8:8 + ts, :] = u
    xc = cbias_ref[...] + cw_ref[3:4, :] * u
    for j in range(CONV_WIDTH - 1):
        xc = xc + cw_ref[j:j + 1, :] * pad_ref[5 + j:5 + j + ts, :]
    tail = pad_ref[ts:ts + 8, :]
    pad_ref[0:8, :] = tail
    cl_ref[0] = tail

    a, b = _lru_coeffs(xc, wa_ref, ba_ref, wx_ref, bx_ref, lam_ref)
    row = lax.broadcasted_iota(jnp.int32, (ts, w), 0)
    d = 1
    while d < ts:
        valid = row >= d
        a_sh = jnp.where(valid, pltpu.roll(a, d, axis=0), 1.0)
        b_sh = jnp.where(valid, pltpu.roll(b, d, axis=0), 0.0)
        b = a * b_sh + b
        a = a * a_sh
        d *= 2
    h = a * h_ref[...] + b
    hl = h[ts - 1:ts, :]
    h_ref[...] = hl
    hl_ref[0] = hl
    o = h * _gelu_tanh(g_ref[0])
    o_ref[0] = _rms_rows(o, onw_ref[...]).astype(o_ref.dtype)


def _dot_nt(a, b):
    return lax.dot_general(a, b, (((1,), (1,)), ((), ())), preferred_element_type=F32)


def _top_blocks_bias(gate, n_valid, slope2, t0):
    nblk, tq = gate.shape
    n_iota = lax.broadcasted_iota(jnp.int32, (nblk, tq), 0)
    ok = n_iota < n_valid
    g = jnp.where(ok, gate, -jnp.inf)
    sel = jnp.zeros((nblk, tq), F32)
    for _ in range(TOP_BLOCKS):
        mx = jnp.max(g, axis=0, keepdims=True)
        first = jnp.min(jnp.where(g == mx, n_iota, nblk), axis=0, keepdims=True)
        hit = n_iota == first
        sel = jnp.where(hit, 1.0, sel)
        g = jnp.where(hit, -jnp.inf, g)
    dist = (n_iota * BLOCK - t0).astype(F32)
    return jnp.where(jnp.logical_and(sel > 0.0, ok), slope2 * dist, NEG)


def _moba_prompt_kernel(q_ref, k_ref, vt_ref, km_ref, sl_ref, o_ref,
                        qs_ref, bias_ref, alibi_ref, acc_ref, m_ref, l_ref):
    qi = pl.program_id(2)
    tq = q_ref.shape[1]
    t0 = qi * tq
    q = q_ref[0]
    km = km_ref[0]
    km_hi, km_lo = _split_bf16(km)
    lane = lax.broadcasted_iota(jnp.int32, q.shape, 1)
    key_i = lax.broadcasted_iota(jnp.int32, (BLOCK, tq), 0)
    qry_i = lax.broadcasted_iota(jnp.int32, (BLOCK, tq), 1)
    causal = key_i <= qry_i
    kd = k_ref[0, pl.ds(pl.multiple_of(t0, BLOCK), BLOCK), :]

    for e in range(2):
        slope2 = sl_ref[0, :, HEAD_DIM * e:HEAD_DIM * e + 1] * LOG2E
        qh = jnp.where((lane // HEAD_DIM) == e, q, 0.0)
        q_hi, q_lo = _split_bf16(qh)
        gate = _dot_nt(km_hi, q_hi) + _dot_nt(km_hi, q_lo) + _dot_nt(km_lo, q_hi)
        bias_ref[e] = _top_blocks_bias(gate, qi, slope2, t0)
        qs = (qh * (HEAD_DIM ** -0.5 * LOG2E)).astype(BF16)
        qs_ref[e] = qs
        alibi = slope2 * key_i.astype(F32)
        alibi_ref[e] = alibi
        s = jnp.where(causal, _dot_nt(kd, qs) + alibi, NEG)
        m = jnp.max(s, axis=0, keepdims=True)
        p = jnp.exp2(s - m)
        m_ref[e] = m
        l_ref[e] = jnp.sum(p, axis=0, keepdims=True)
        vt = vt_ref[0, HEAD_DIM * e:HEAD_DIM * (e + 1), pl.ds(pl.multiple_of(t0, BLOCK), BLOCK)]
        acc_ref[e] = jnp.dot(vt, p.astype(BF16), preferred_element_type=F32)

    def body(n, carry):
        base = pl.multiple_of(n * BLOCK, BLOCK)
        kn = k_ref[0, pl.ds(base, BLOCK), :]
        for e in range(2):
            s = _dot_nt(kn, qs_ref[e]) + (alibi_ref[e] + bias_ref[e, pl.ds(n, 1), :])
            m_old = m_ref[e]
            m_new = jnp.maximum(m_old, jnp.max(s, axis=0, keepdims=True))
            alpha = jnp.exp2(m_old - m_new)
            p = jnp.exp2(s - m_new)
            m_ref[e] = m_new
            l_ref[e] = alpha * l_ref[e] + jnp.sum(p, axis=0, keepdims=True)
            vt = vt_ref[0, HEAD_DIM * e:HEAD_DIM * (e + 1), pl.ds(base, BLOCK)]
            acc_ref[e] = alpha * acc_ref[e] + jnp.dot(vt, p.astype(BF16), preferred_element_type=F32)
        return carry

    lax.fori_loop(0, qi, body, 0)
    o_t = jnp.concatenate([acc_ref[e] / l_ref[e] for e in range(2)], axis=0)
    o_ref[0] = o_t.T


def moba_prompt(q, kb, vt, kmean, slopes_lane):
    b, s, aw = q.shape
    nblk = s // BLOCK
    hp = aw // 128
    return pl.pallas_call(
        _moba_prompt_kernel,
        grid=(b, hp, nblk),
        in_specs=[pl.BlockSpec((1, BLOCK, 128), lambda i, h, j: (i, j, h)),
                  pl.BlockSpec((1, s, 128), lambda i, h, j: (i, 0, h)),
                  pl.BlockSpec((1, 128, s), lambda i, h, j: (i, h, 0)),
                  pl.BlockSpec((1, nblk, 128), lambda i, h, j: (i, 0, h)),
                  pl.BlockSpec((1, 1, 128), lambda i, h, j: (h, 0, 0))],
        out_specs=pl.BlockSpec((1, BLOCK, 128), lambda i, h, j: (i, j, h)),
        out_shape=jax.ShapeDtypeStruct((b, s, aw), F32),
        scratch_shapes=[pltpu.VMEM((2, BLOCK, 128), BF16), pltpu.VMEM((2, nblk, BLOCK), F32),
                        pltpu.VMEM((2, BLOCK, BLOCK), F32), pltpu.VMEM((2, HEAD_DIM, BLOCK), F32),
                        pltpu.VMEM((2, 1, BLOCK), F32), pltpu.VMEM((2, 1, BLOCK), F32)],
        compiler_params=_cparams(("arbitrary", "arbitrary", "arbitrary")),
        name="moba_prompt",
    )(q, kb, vt, kmean, slopes_lane)


def _alibi_slopes_lane():
    sl = np.exp2(-8.0 * np.arange(1, N_HEADS + 1) / N_HEADS).astype(np.float32)
    return jnp.asarray(np.repeat(sl, HEAD_DIM).reshape(N_HEADS * HEAD_DIM // 128, 1, 128))


ROUTE_LANES = 128


def _route(logits):
    lane = lax.broadcasted_iota(jnp.int32, logits.shape, 1)
    ninf = -jnp.inf
    gl = jnp.where(lane < N_GROUPS, logits, ninf)
    gmax = jnp.max(gl, axis=-1, keepdims=True)
    gsel = jnp.min(jnp.where(gl == gmax, lane, ROUTE_LANES), axis=-1, keepdims=True)
    p_sel = 1.0 / jnp.sum(jnp.exp(gl - gmax), axis=-1, keepdims=True)
    lo = N_GROUPS + EXPERTS_PER_GROUP * gsel
    emask = jnp.logical_and(lane >= lo, lane < lo + EXPERTS_PER_GROUP)
    el = jnp.where(emask, logits, ninf)
    emax = jnp.max(el, axis=-1, keepdims=True)
    pe_un = jnp.exp(el - emax)
    pe = jnp.where(emask, pe_un / jnp.sum(pe_un, axis=-1, keepdims=True), -1.0)
    w1 = jnp.max(pe, axis=-1, keepdims=True)
    l1 = jnp.min(jnp.where(pe == w1, lane, ROUTE_LANES), axis=-1, keepdims=True)
    pe2 = jnp.where(lane == l1, -1.0, pe)
    w2 = jnp.max(pe2, axis=-1, keepdims=True)
    l2 = jnp.min(jnp.where(pe2 == w2, lane, ROUTE_LANES), axis=-1, keepdims=True)
    tot = w1 + w2
    out = jnp.where(lane == 0, (l1 - N_GROUPS).astype(F32), 0.0)
    out = jnp.where(lane == 1, (l2 - N_GROUPS).astype(F32), out)
    out = jnp.where(lane == 2, p_sel * (w1 / tot), out)
    out = jnp.where(lane == 3, p_sel * (w2 / tot), out)
    return out


def _outproj_kernel(oa_ref, or_ref, x_ref, anw_ref, wo_ref, fnw_ref, rwh_ref, rwl_ref, rb_ref,
                    xm_ref, xn_ref, rt_ref):
    a_n = _rms_rows(oa_ref[...], anw_ref[...]).astype(BF16)
    mix = (jnp.dot(a_n, wo_ref[0:ATTN_WIDTH, :], preferred_element_type=F32)
           + jnp.dot(or_ref[...], wo_ref[ATTN_WIDTH:, :], preferred_element_type=F32))
    xm = x_ref[...] + mix
    xm_ref[...] = xm
    xn = _rms_rows(xm, fnw_ref[...])
    xn_ref[...] = xn
    hi, lo = _split_bf16(xn)
    logits = (jnp.dot(hi, rwh_ref[...], preferred_element_type=F32)
              + jnp.dot(lo, rwh_ref[...], preferred_element_type=F32)
              + jnp.dot(hi, rwl_ref[...], preferred_element_type=F32)) + rb_ref[...]
    rt_ref[...] = _route(logits)


def out_proj(o_attn, o_rnn_n, x2d, attn_out_norm, w_out_bf, norm_ffn, rgw, rgb, rew, reb, tm):
    n = x2d.shape[0]
    assert n % tm == 0
    rw = jnp.zeros((D_MODEL, ROUTE_LANES), F32).at[:, :N_GROUPS].set(rgw).at[:, N_GROUPS:N_GROUPS + N_EXPERTS].set(rew)
    rb = jnp.zeros((1, ROUTE_LANES), F32).at[0, :N_GROUPS].set(rgb).at[0, N_GROUPS:N_GROUPS + N_EXPERTS].set(reb)
    rwh, rwl = _split_bf16(rw)
    row = lambda w: pl.BlockSpec((tm, w), lambda i: (i, 0))
    const = lambda shp: pl.BlockSpec(shp, lambda i: (0, 0))
    return pl.pallas_call(
        _outproj_kernel,
        grid=(n // tm,),
        in_specs=[row(ATTN_WIDTH), row(RNN_WIDTH), row(D_MODEL), const((1, ATTN_WIDTH)),
                  const((D_MODEL, D_MODEL)), const((1, D_MODEL)), const((D_MODEL, ROUTE_LANES)),
                  const((D_MODEL, ROUTE_LANES)), const((1, ROUTE_LANES))],
        out_specs=[row(D_MODEL), row(D_MODEL), row(ROUTE_LANES)],
        out_shape=[jax.ShapeDtypeStruct((n, D_MODEL), F32), jax.ShapeDtypeStruct((n, D_MODEL), F32),
                   jax.ShapeDtypeStruct((n, ROUTE_LANES), F32)],
        compiler_params=_cparams(("arbitrary",)),
        name="out_proj",
    )(o_attn, o_rnn_n, x2d, attn_out_norm.reshape(1, ATTN_WIDTH), w_out_bf, norm_ffn.reshape(1, D_MODEL),
      rwh, rwl, rb)


MOE_ROWS = 256


def _gather_rows(src_hbm, dst, sem, idx_ref, base, count):
    def issue(r, c):
        pltpu.make_async_copy(src_hbm.at[pl.ds(idx_ref[base + r], 1), :], dst.at[pl.ds(r, 1), :], sem).start()
        return c
    lax.fori_loop(0, count, issue, 0, unroll=8)


def _wait_rows(src_hbm, dst, sem, count):
    pltpu.make_async_copy(src_hbm.at[pl.ds(0, count), :], dst, sem).wait()


def _expert_kernel(be_ref, nu_ref, tok_ref, x_hbm, wg_ref, wu_ref, wd_ref, y_ref,
                   xbuf, sems, wg_bf, wu_bf, wd_bf):
    i = pl.program_id(0)
    n_used = nu_ref[0]
    slot = lax.rem(i, 2)

    @pl.when(i == 0)
    def _():
        _gather_rows(x_hbm, xbuf.at[0], sems.at[0], tok_ref, 0, MOE_ROWS)

    @pl.when(i + 1 < n_used)
    def _():
        _gather_rows(x_hbm, xbuf.at[1 - slot], sems.at[1 - slot], tok_ref, (i + 1) * MOE_ROWS, MOE_ROWS)

    changed = jnp.logical_or(i == 0, be_ref[i] != be_ref[jnp.maximum(i - 1, 0)])

    @pl.when(jnp.logical_and(changed, i < n_used))
    def _():
        wg_bf[...] = wg_ref[0].astype(BF16)
        wu_bf[...] = wu_ref[0].astype(BF16)
        wd_bf[...] = wd_ref[0].astype(BF16)

    @pl.when(i < n_used)
    def _():
        _wait_rows(x_hbm, xbuf.at[slot], sems.at[slot], MOE_ROWS)
        xb = xbuf[slot].astype(BF16)
        hg = jnp.dot(xb, wg_bf[...], preferred_element_type=F32)
        hu = jnp.dot(xb, wu_bf[...], preferred_element_type=F32)
        hdn = (hg * jax.nn.sigmoid(hg) * hu).astype(BF16)
        y_ref[...] = jnp.dot(hdn, wd_bf[...], preferred_element_type=F32)

    @pl.when(i >= n_used)
    def _():
        y_ref[...] = jnp.zeros_like(y_ref)


def expert_ffn(xn, block_expert, n_used, buf_tok, w_gate, w_up, w_down):
    nb = block_expert.shape[0]
    wmap = lambda i, be, nu, tok: (be[i], 0, 0)
    return pl.pallas_call(
        _expert_kernel,
        grid_spec=pltpu.PrefetchScalarGridSpec(
            num_scalar_prefetch=3,
            grid=(nb,),
            in_specs=[pl.BlockSpec(memory_space=pl.ANY),
                      pl.BlockSpec((1, D_MODEL, D_EXPERT), wmap),
                      pl.BlockSpec((1, D_MODEL, D_EXPERT), wmap),
                      pl.BlockSpec((1, D_EXPERT, D_MODEL), wmap)],
            out_specs=pl.BlockSpec((MOE_ROWS, D_MODEL), lambda i, be, nu, tok: (i, 0)),
            scratch_shapes=[pltpu.VMEM((2, MOE_ROWS, D_MODEL), F32), pltpu.SemaphoreType.DMA((2,)),
                            pltpu.VMEM((D_MODEL, D_EXPERT), BF16), pltpu.VMEM((D_MODEL, D_EXPERT), BF16),
                            pltpu.VMEM((D_EXPERT, D_MODEL), BF16)]),
        out_shape=jax.ShapeDtypeStruct((nb * MOE_ROWS, D_MODEL), F32),
        compiler_params=_cparams(("arbitrary",)),
        name="expert_ffn",
    )(block_expert, n_used, buf_tok, xn, w_gate, w_up, w_down)


COMBINE_ROWS = 256


def _combine_kernel(pos_ref, y_hbm, xm_ref, rt_ref, o_ref, ybuf, sems):
    i = pl.program_id(0)
    nsteps = pl.num_programs(0)
    slot = lax.rem(i, 2)
    rows = 2 * COMBINE_ROWS

    @pl.when(i == 0)
    def _():
        _gather_rows(y_hbm, ybuf.at[0], sems.at[0], pos_ref, 0, rows)

    @pl.when(i + 1 < nsteps)
    def _():
        _gather_rows(y_hbm, ybuf.at[1 - slot], sems.at[1 - slot], pos_ref, (i + 1) * rows, rows)

    _wait_rows(y_hbm, ybuf.at[slot], sems.at[slot], rows)
    rt = rt_ref[...]
    o_ref[...] = (xm_ref[...] + (rt[:, 2:3] * ybuf[slot, 0:COMBINE_ROWS, :]
                                 + rt[:, 3:4] * ybuf[slot, COMBINE_ROWS:rows, :]))


def moe_combine(pos_flat, yb, x_mid, route):
    n = x_mid.shape[0]
    assert n % COMBINE_ROWS == 0
    row = lambda w: pl.BlockSpec((COMBINE_ROWS, w), lambda i, pos: (i, 0))
    return pl.pallas_call(
        _combine_kernel,
        grid_spec=pltpu.PrefetchScalarGridSpec(
            num_scalar_prefetch=1,
            grid=(n // COMBINE_ROWS,),
            in_specs=[pl.BlockSpec(memory_space=pl.ANY), row(D_MODEL), row(ROUTE_LANES)],
            out_specs=row(D_MODEL),
            scratch_shapes=[pltpu.VMEM((2, 2 * COMBINE_ROWS, D_MODEL), F32), pltpu.SemaphoreType.DMA((2,))]),
        out_shape=jax.ShapeDtypeStruct((n, D_MODEL), F32),
        compiler_params=_cparams(("arbitrary",)),
        name="moe_combine",
    )(pos_flat, yb, x_mid, route)


def _moe_plan(route, n):
    ids = route[:, 0:2].astype(jnp.int32)
    a = 2 * n
    flat_e = ids.reshape(a)
    order = jnp.argsort(flat_e, stable=True)
    se = flat_e[order]
    counts = jnp.bincount(flat_e, length=N_EXPERTS)
    padded = (counts + MOE_ROWS - 1) // MOE_ROWS * MOE_ROWS
    pad_end = jnp.cumsum(padded)
    pad_start = pad_end - padded
    start = jnp.cumsum(counts) - counts
    dest = (pad_start[se] + jnp.arange(a) - start[se]).astype(jnp.int32)
    nb = -(-a // MOE_ROWS) + N_EXPERTS
    buf_tok = jnp.zeros((nb * MOE_ROWS,), jnp.int32).at[dest].set((order // 2).astype(jnp.int32))
    block_expert = jnp.clip(jnp.searchsorted(pad_end, jnp.arange(nb) * MOE_ROWS, side='right'),
                            0, N_EXPERTS - 1).astype(jnp.int32)
    n_used = (pad_end[-1] // MOE_ROWS).astype(jnp.int32).reshape(1)
    pos = jnp.zeros((a,), jnp.int32).at[order].set(dest).reshape(n // COMBINE_ROWS, COMBINE_ROWS, 2)
    pos_flat = jnp.swapaxes(pos, 1, 2).reshape(a)
    return block_expert, n_used, buf_tok, pos_flat


def moe_layer(x_mid, xn, route, w_gate, w_up, w_down):
    n = x_mid.shape[0]
    block_expert, n_used, buf_tok, pos_flat = _moe_plan(route, n)
    yb = expert_ffn(xn, block_expert, n_used, buf_tok, w_gate, w_up, w_down)
    return moe_combine(pos_flat, yb, x_mid, route)


PAGE_CHUNK = 8


def _moba_sample_kernel(pt_ref, q_ref, kn_ref, vn_ref, ck_hbm, cv_hbm, o_ref,
                        buf, sems, s_ref, p_ref, ksum_ref, acc_ref, *, n_pages, page_rows, t_len):
    b = pl.program_id(0)
    nb = pl.num_programs(0)
    nh = N_HEADS
    rows = q_ref.shape[1]
    flat = page_rows * nh
    n_chunks = n_pages // PAGE_CHUNK
    ppb = BLOCK // page_rows
    n_blocks = n_pages // ppb
    blk_cols = ppb * flat
    past = n_pages * page_rows

    def issue(cache, seq, chunk, slot):
        for j in range(PAGE_CHUNK):
            pg = pt_ref[seq * n_pages + chunk * PAGE_CHUNK + j]
            pltpu.make_async_copy(cache.at[0, pg], buf.at[slot, j], sems.at[slot]).start()

    def wait(cache, slot):
        pltpu.make_async_copy(cache.at[0, pl.ds(0, PAGE_CHUNK)], buf.at[slot], sems.at[slot]).wait()

    @pl.when(b == 0)
    def _():
        issue(ck_hbm, 0, 0, 0)

    q = q_ref[0]
    qs = (q * (HEAD_DIM ** -0.5 * LOG2E)).astype(BF16)

    def k_chunk(c, carry):
        slot = lax.rem(c, 2)

        @pl.when(c + 1 < n_chunks)
        def _():
            issue(ck_hbm, b, c + 1, 1 - slot)

        @pl.when(c + 1 == n_chunks)
        def _():
            issue(cv_hbm, b, 0, 1 - slot)

        wait(ck_hbm, slot)

        def page(j, cc):
            pg = c * PAGE_CHUNK + j
            kp = buf[slot, j]
            ksum_ref[pg] = jnp.sum(kp, axis=0)
            kf = kp.reshape(flat, HEAD_DIM).astype(BF16)
            s_ref[:, pl.ds(pl.multiple_of(pg * flat, flat), flat)] = _dot_nt(qs, kf)
            return cc
        return lax.fori_loop(0, PAGE_CHUNK, page, carry)

    lax.fori_loop(0, n_chunks, k_chunk, 0)

    ksum = ksum_ref[...].reshape(n_blocks, ppb, nh, HEAD_DIM)
    kmean = (jnp.sum(ksum, axis=1) * (1.0 / BLOCK)).reshape(n_blocks * nh, HEAD_DIM)
    q_hi, q_lo = _split_bf16(q)
    km_hi, km_lo = _split_bf16(kmean)
    gate = _dot_nt(q_hi, km_hi) + _dot_nt(q_lo, km_hi) + _dot_nt(q_hi, km_lo)
    g_row = lax.broadcasted_iota(jnp.int32, gate.shape, 0)
    g_col = lax.broadcasted_iota(jnp.int32, gate.shape, 1)
    g = jnp.where(lax.rem(g_col, nh) == lax.rem(g_row, nh), gate, -jnp.inf)
    sel = jnp.zeros(gate.shape, F32)
    for _ in range(min(TOP_BLOCKS, n_blocks)):
        mx = jnp.max(g, axis=-1, keepdims=True)
        first = jnp.min(jnp.where(g == mx, g_col, n_blocks * nh), axis=-1, keepdims=True)
        hit = g_col == first
        sel = jnp.where(hit, 1.0, sel)
        g = jnp.where(hit, -jnp.inf, g)

    r_col = lax.broadcasted_iota(jnp.int32, (rows, 1), 0)
    head_r = lax.rem(r_col, nh)
    tok_r = (r_col // nh).astype(F32)
    slope2 = lax.bitcast_convert_type(lax.shift_left(126 - head_r, 23), F32) * LOG2E
    col = lax.broadcasted_iota(jnp.int32, (rows, blk_cols), 1)
    own = lax.rem(col, nh) == head_r
    key_in_blk = (col // nh).astype(F32)
    base_bias = jnp.where(own, slope2 * key_in_blk, NEG)

    def blk_bias(n):
        chosen = jnp.sum(sel[:, n * nh:(n + 1) * nh], axis=-1, keepdims=True) > 0.0
        return jnp.where(chosen, slope2 * (float(n * BLOCK - past) - tok_r), NEG)

    kn = kn_ref[0]
    s_loc = _dot_nt(qs, kn.astype(BF16))
    l_row = lax.broadcasted_iota(jnp.int32, s_loc.shape, 0)
    l_col = lax.broadcasted_iota(jnp.int32, s_loc.shape, 1)
    l_ok = jnp.logical_and(lax.rem(l_col, nh) == lax.rem(l_row, nh), l_col // nh <= l_row // nh)
    s_loc = jnp.where(l_ok, s_loc + slope2 * ((l_col // nh).astype(F32) - tok_r), NEG)

    m = jnp.max(s_loc, axis=-1, keepdims=True)
    for n in range(n_blocks):
        sb = s_ref[:, n * blk_cols:(n + 1) * blk_cols] + (base_bias + blk_bias(n))
        m = jnp.maximum(m, jnp.max(sb, axis=-1, keepdims=True))
    p_loc = jnp.exp2(s_loc - m)
    l = jnp.sum(p_loc, axis=-1, keepdims=True)
    for n in range(n_blocks):
        sb = s_ref[:, n * blk_cols:(n + 1) * blk_cols] + (base_bias + blk_bias(n))
        pb = jnp.exp2(sb - m)
        l = l + jnp.sum(pb, axis=-1, keepdims=True)
        p_ref[:, n * blk_cols:(n + 1) * blk_cols] = pb.astype(BF16)
    acc_ref[...] = jnp.dot(p_loc.astype(BF16), vn_ref[0].astype(BF16), preferred_element_type=F32)

    def v_chunk(c, carry):
        slot = lax.rem(c, 2)

        @pl.when(c + 1 < n_chunks)
        def _():
            issue(cv_hbm, b, c + 1, 1 - slot)

        @pl.when(jnp.logical_and(c + 1 == n_chunks, b + 1 < nb))
        def _():
            issue(ck_hbm, b + 1, 0, 1 - slot)

        wait(cv_hbm, slot)

        def page(j, cc):
            pg = c * PAGE_CHUNK + j
            vf = buf[slot, j].reshape(flat, HEAD_DIM).astype(BF16)
            pp = p_ref[:, pl.ds(pl.multiple_of(pg * flat, flat), flat)]
            acc_ref[...] += jnp.dot(pp, vf, preferred_element_type=F32)
            return cc
        return lax.fori_loop(0, PAGE_CHUNK, page, carry)

    lax.fori_loop(0, n_chunks, v_chunk, 0)
    o_ref[0] = acc_ref[...] / l


def moba_sample(q, k, v, cache_k, cache_v, page_table):
    db, t_len, aw = q.shape
    n_pages = page_table.shape[1]
    page_rows = cache_k.shape[2]
    assert n_pages % (2 * PAGE_CHUNK) == 0 and BLOCK % page_rows == 0
    rows = t_len * N_HEADS
    flat = page_rows * N_HEADS
    fl = lambda a: a.reshape(db, rows, HEAD_DIM)
    per_seq = pl.BlockSpec((1, rows, HEAD_DIM), lambda i, pt: (i, 0, 0))
    kern = functools.partial(_moba_sample_kernel, n_pages=n_pages, page_rows=page_rows, t_len=t_len)
    o = pl.pallas_call(
        kern,
        grid_spec=pltpu.PrefetchScalarGridSpec(
            num_scalar_prefetch=1,
            grid=(db,),
            in_specs=[per_seq, per_seq, per_seq, pl.BlockSpec(memory_space=pl.ANY),
                      pl.BlockSpec(memory_space=pl.ANY)],
            out_specs=per_seq,
            scratch_shapes=[pltpu.VMEM((2, PAGE_CHUNK, page_rows, N_HEADS, HEAD_DIM), F32),
                            pltpu.SemaphoreType.DMA((2,)),
                            pltpu.VMEM((rows, n_pages * flat), F32),
                            pltpu.VMEM((rows, n_pages * flat), BF16),
                            pltpu.VMEM((n_pages, N_HEADS, HEAD_DIM), F32),
                            pltpu.VMEM((rows, HEAD_DIM), F32)]),
        out_shape=jax.ShapeDtypeStruct((db, rows, HEAD_DIM), F32),
        compiler_params=_cparams(("arbitrary",)),
        name="moba_sample",
    )(page_table.reshape(-1), fl(q), fl(k), fl(v), cache_k, cache_v)
    return o.reshape(db, t_len, aw)


def _blockdiag(w):
    nb, c, d = w.shape
    eye = jnp.eye(nb, dtype=w.dtype)
    return (eye[:, None, :, None] * w[:, :, None, :]).reshape(nb * c, nb * d)


def _rnn_weights(conv_w, conv_b, lru_wa, lru_ba, lru_wx, lru_bx, lru_lambda, rnn_out_norm):
    r1 = lambda a: a.reshape(1, RNN_WIDTH)
    return (conv_w, r1(conv_b), _blockdiag(lru_wa).astype(BF16), r1(lru_ba),
            _blockdiag(lru_wx).astype(BF16), r1(lru_bx), r1(lru_lambda), r1(rnn_out_norm))


def rnn_seq(u, g, conv_buf, h0, weights, ts):
    b, s, w = u.shape
    assert s % ts == 0 and ts % 8 == 0
    cb8 = jnp.concatenate([jnp.zeros((b, 8 - (CONV_WIDTH - 1), w), F32), conv_buf], axis=1)
    tile = pl.BlockSpec((1, ts, w), lambda i, j: (i, j, 0))
    perb = lambda r: pl.BlockSpec((1, r, w), lambda i, j: (i, 0, 0))
    const = lambda shp: pl.BlockSpec(shp, lambda i, j: (0, 0))
    wspecs = [const((CONV_WIDTH, w)), const((1, w)), const((w, w)), const((1, w)), const((w, w)),
              const((1, w)), const((1, w)), const((1, w))]
    o, hl, cl = pl.pallas_call(
        _rnn_seq_kernel,
        grid=(b, s // ts),
        in_specs=[tile, tile, perb(8), perb(1)] + wspecs,
        out_specs=[tile, perb(1), perb(8)],
        out_shape=[jax.ShapeDtypeStruct((b, s, w), BF16), jax.ShapeDtypeStruct((b, 1, w), F32),
                   jax.ShapeDtypeStruct((b, 8, w), F32)],
        scratch_shapes=[pltpu.VMEM((ts + 8, w), F32), pltpu.VMEM((1, w), F32)],
        compiler_params=_cparams(("arbitrary", "arbitrary")),
        name="rnn_seq",
    )(u, g, cb8, h0.reshape(b, 1, w), *weights)
    return o, hl.reshape(b, w), cl[:, 8 - (CONV_WIDTH - 1):, :]


def _rnn_step_kernel(u_ref, g_ref, cb_ref, h0_ref, cw_ref, cbias_ref, wa_ref, ba_ref, wx_ref, bx_ref,
                     lam_ref, onw_ref, o_ref, hl_ref, cl_ref):
    t_len = u_ref.shape[0]
    taps = [cb_ref[j] for j in range(CONV_WIDTH - 1)] + [u_ref[t] for t in range(t_len)]
    h = h0_ref[...]
    for t in range(t_len):
        xc = cbias_ref[...]
        for j in range(CONV_WIDTH):
            xc = xc + cw_ref[j:j + 1, :] * taps[t + j]
        a, b = _lru_coeffs(xc, wa_ref, ba_ref, wx_ref, bx_ref, lam_ref)
        h = a * h + b
        o_ref[t] = _rms_rows(h * _gelu_tanh(g_ref[t]), onw_ref[...]).astype(o_ref.dtype)
    hl_ref[...] = h
    for j in range(CONV_WIDTH - 1):
        cl_ref[j] = taps[t_len + j]


def rnn_step(u, g, conv_buf, h0, weights):
    db, t_len, w = u.shape
    tm = lambda a: jnp.swapaxes(a, 0, 1)
    o, hl, cl = pl.pallas_call(
        _rnn_step_kernel,
        out_shape=[jax.ShapeDtypeStruct((t_len, db, w), BF16), jax.ShapeDtypeStruct((db, w), F32),
                   jax.ShapeDtypeStruct((CONV_WIDTH - 1, db, w), F32)],
        compiler_params=pltpu.CompilerParams(vmem_limit_bytes=VMEM_LIMIT),
        name="rnn_step",
    )(tm(u), tm(g), tm(conv_buf), h0, *weights)
    return tm(o), hl, tm(cl)


ROW_TILE = 512
SCAN_TILE = 256


def _trunk_layer(x, attn_fn, rnn_fn, lp):
    b, t, d = x.shape
    n = b * t
    x2d = x.reshape(n, d)
    q, k, v, kb, vb, u, g, kmean = in_proj(x2d, lp["norm_mix"], lp["w_in"], lp["q_norm"], lp["k_norm"],
                                           min(ROW_TILE, n))
    sh = lambda a: a.reshape(b, t, -1)
    o_attn = attn_fn(sh(q), sh(k), sh(v), sh(kb), sh(vb), kmean)
    o_rnn, h_last, conv_last = rnn_fn(sh(u), sh(g))
    x_mid, xn, route = out_proj(o_attn.reshape(n, -1), o_rnn.reshape(n, -1), x2d, lp["attn_out_norm"],
                                lp["w_out"], lp["norm_ffn"], lp["rgw"], lp["rgb"], lp["rew"], lp["reb"],
                                min(ROW_TILE, n))
    y = moe_layer(x_mid, xn, route, lp["egate"], lp["eup"], lp["edown"])
    heads = lambda a: a.reshape(b, t, N_HEADS, HEAD_DIM)
    return y.reshape(b, t, d), heads(k), heads(v), conv_last, h_last


def kernel(x_prompt, x_sample, cache_k, cache_v, state_conv, state_h, page_table, norm_mix, w_in, q_norm, k_norm, conv_w, conv_b, lru_wa, lru_ba, lru_wx, lru_bx, lru_lambda, attn_out_norm, rnn_out_norm, w_out, norm_ffn, router_group_w, router_group_b, router_expert_w, router_expert_b, expert_w_gate, expert_w_up, expert_w_down):
    depth = w_in.shape[0]
    slopes_lane = _alibi_slopes_lane()
    yp, ys = x_prompt, x_sample
    outs = [[] for _ in range(8)]
    for l in range(depth):
        lp = dict(norm_mix=norm_mix[l], w_in=w_in[l].astype(BF16), q_norm=q_norm[l], k_norm=k_norm[l],
                  attn_out_norm=attn_out_norm[l], w_out=w_out[l].astype(BF16), norm_ffn=norm_ffn[l],
                  rgw=router_group_w[l], rgb=router_group_b[l], rew=router_expert_w[l], reb=router_expert_b[l],
                  egate=expert_w_gate[l], eup=expert_w_up[l], edown=expert_w_down[l])
        rnn_w = _rnn_weights(conv_w[l], conv_b[l], lru_wa[l], lru_ba[l], lru_wx[l], lru_bx[l],
                             lru_lambda[l], rnn_out_norm[l])
        bp = yp.shape[0]

        def prompt_attn(q, k, v, kb, vb, kmean):
            return moba_prompt(q, kb, jnp.swapaxes(vb, 1, 2), kmean.reshape(bp, -1, ATTN_WIDTH), slopes_lane)

        def prompt_rnn(u, g):
            return rnn_seq(u, g, jnp.zeros((bp, CONV_WIDTH - 1, RNN_WIDTH), F32),
                           jnp.zeros((bp, RNN_WIDTH), F32), rnn_w, SCAN_TILE)

        def sample_attn(q, k, v, kb, vb, kmean, l=l):
            return moba_sample(q, k, v, cache_k[l:l + 1], cache_v[l:l + 1], page_table)

        def sample_rnn(u, g, l=l):
            return rnn_step(u, g, state_conv[l], state_h[l], rnn_w)

        yp, kp, vp, cp, hp = _trunk_layer(yp, prompt_attn, prompt_rnn, lp)
        ys, ks_, vs_, cs, hs = _trunk_layer(ys, sample_attn, sample_rnn, lp)
        for lst, val in zip(outs, (kp, vp, cp, hp, ks_, vs_, cs, hs)):
            lst.append(val)
    return (yp, ys) + tuple(jnp.stack(lst) for lst in outs)
```

```python
import functools

import jax
import jax.numpy as jnp
import numpy as np
from jax import lax
from jax.experimental import pallas as pl
from jax.experimental.pallas import tpu as pltpu

F32 = jnp.float32
BF16 = jnp.bfloat16

D_MODEL = 1024
ATTN_WIDTH = 512
RNN_WIDTH = 512
HEAD_DIM = 64
N_HEADS = 8
LRU_BLOCKS = 8
CONV_WIDTH = 4
LRU_C = 8.0
BLOCK = 256
TOP_BLOCKS = 3
N_GROUPS = 4
EXPERTS_PER_GROUP = 8
N_EXPERTS = 32
D_EXPERT = 512
EPS = 1e-6
D_IN = ATTN_WIDTH + 2 * ATTN_WIDTH + 2 * RNN_WIDTH
NEG = -1e30
LOG2E = 1.4426950408889634

VMEM_LIMIT = 56 * 1024 * 1024


def _cparams(sem):
    return pltpu.CompilerParams(dimension_semantics=sem, vmem_limit_bytes=VMEM_LIMIT)


def _split_bf16(a):
    hi = a.astype(BF16)
    lo = (a - hi.astype(F32)).astype(BF16)
    return hi, lo


def _inproj_kernel(x_ref, nw_ref, w_ref, qn_ref, kn_ref, gm_ref,
                   q_ref, k_ref, v_ref, kb_ref, vb_ref, u_ref, g_ref, km_ref):
    x = x_ref[...]
    ms = jnp.mean(x * x, axis=-1, keepdims=True)
    xn = x * lax.rsqrt(ms + EPS) * nw_ref[...]
    z = jnp.dot(xn.astype(BF16), w_ref[...], preferred_element_type=F32)
    aw = ATTN_WIDTH

    def head_norm(t, gain):
        hi, lo = _split_bf16(t * t)
        hm = (jnp.dot(hi, gm_ref[...], preferred_element_type=F32)
              + jnp.dot(lo, gm_ref[...], preferred_element_type=F32))
        return t * lax.rsqrt(hm + EPS) * gain

    q = head_norm(z[:, 0:aw], qn_ref[...])
    k = head_norm(z[:, aw:2 * aw], kn_ref[...])
    v = z[:, 2 * aw:3 * aw]
    q_ref[...] = q
    k_ref[...] = k
    v_ref[...] = v
    kb_ref[...] = k.astype(BF16)
    vb_ref[...] = v.astype(BF16)
    u_ref[...] = z[:, 3 * aw:3 * aw + RNN_WIDTH]
    g_ref[...] = z[:, 3 * aw + RNN_WIDTH:]
    tm = k.shape[0]
    km_ref[0] = jnp.mean(k.reshape(tm // BLOCK, BLOCK, aw), axis=1)


def _head_avg_matrix():
    r = jnp.arange(ATTN_WIDTH) // HEAD_DIM
    return jnp.where(r[:, None] == r[None, :], 1.0 / HEAD_DIM, 0.0).astype(BF16)


def in_proj(x2d, norm_w, w_in_bf, q_norm, k_norm, tm):
    n = x2d.shape[0]
    assert n % tm == 0 and tm % BLOCK == 0
    nt = n // tm
    row = lambda i: (i, 0)
    const = lambda i: (0, 0)
    wspec = lambda shp: pl.BlockSpec(shp, const)
    out_f = jax.ShapeDtypeStruct((n, ATTN_WIDTH), F32)
    out_b = jax.ShapeDtypeStruct((n, ATTN_WIDTH), BF16)
    ospec = pl.BlockSpec((tm, ATTN_WIDTH), row)
    return pl.pallas_call(
        _inproj_kernel,
        grid=(nt,),
        in_specs=[pl.BlockSpec((tm, D_MODEL), row), wspec((1, D_MODEL)), wspec((D_MODEL, D_IN)),
                  wspec((1, ATTN_WIDTH)), wspec((1, ATTN_WIDTH)), wspec((ATTN_WIDTH, ATTN_WIDTH))],
        out_specs=[ospec, ospec, ospec, ospec, ospec, ospec, ospec,
                   pl.BlockSpec((1, tm // BLOCK, ATTN_WIDTH), lambda i: (i, 0, 0))],
        out_shape=[out_f, out_f, out_f, out_b, out_b, out_f, out_f,
                   jax.ShapeDtypeStruct((nt, tm // BLOCK, ATTN_WIDTH), F32)],
        compiler_params=_cparams(("arbitrary",)),
        name="in_proj",
    )(x2d, norm_w.reshape(1, D_MODEL), w_in_bf,
      jnp.tile(q_norm, N_HEADS).reshape(1, ATTN_WIDTH), jnp.tile(k_norm, N_HEADS).reshape(1, ATTN_WIDTH),
      _head_avg_matrix())


def _gelu_tanh(x):
    return 0.5 * x * (1.0 + jnp.tanh(0.7978845608028654 * (x + 0.044715 * (x * x * x))))


def _softplus(z):
    return jnp.maximum(z, 0.0) + jnp.log1p(jnp.exp(-jnp.abs(z)))


def _lru_coeffs(xc, wa_ref, ba_ref, wx_ref, bx_ref, lam_ref):
    xb = xc.astype(BF16)
    r = jax.nn.sigmoid(jnp.dot(xb, wa_ref[...], preferred_element_type=F32) + ba_ref[...])
    i = jax.nn.sigmoid(jnp.dot(xb, wx_ref[...], preferred_element_type=F32) + bx_ref[...])
    log_a = -LRU_C * r * _softplus(-lam_ref[...])
    a = jnp.exp(log_a)
    b = jnp.sqrt(-jnp.tanh(log_a) * (a * a + 1.0)) * i * xc
    return a, b


def _rms_rows(x, gain):
    return x * lax.rsqrt(jnp.mean(x * x, axis=-1, keepdims=True) + EPS) * gain


def _rnn_seq_kernel(u_ref, g_ref, cb_ref, h0_ref, cw_ref, cbias_ref, wa_ref, ba_ref, wx_ref, bx_ref,
                    lam_ref, onw_ref, o_ref, hl_ref, cl_ref, pad_ref, h_ref):
    s = pl.program_id(1)
    ts = u_ref.shape[1]
    w = u_ref.shape[2]

    @pl.when(s == 0)
    def _():
        pad_ref[0:8, :] = cb_ref[0]
        h_ref[...] = h0_ref[0]

    u = u_ref[0]
    pad_ref[8:8 + ts, :] = u
    xc = cbias_ref[...] + cw_ref[3:4, :] * u
    for j in range(CONV_WIDTH - 1):
        xc = xc + cw_ref[j:j + 1, :] * pad_ref[5 + j:5 + j + ts, :]
    tail = pad_ref[ts:ts + 8, :]
    pad_ref[0:8, :] = tail
    cl_ref[0] = tail

    a, b = _lru_coeffs(xc, wa_ref, ba_ref, wx_ref, bx_ref, lam_ref)
    row = lax.broadcasted_iota(jnp.int32, (ts, w), 0)
    d = 1
    while d < ts:
        valid = row >= d
        a_sh = jnp.where(valid, pltpu.roll(a, d, axis=0), 1.0)
        b_sh = jnp.where(valid, pltpu.roll(b, d, axis=0), 0.0)
        b = a * b_sh + b
        a = a * a_sh
        d *= 2
    h = a * h_ref[...] + b
    hl = h[ts - 1:ts, :]
    h_ref[...] = hl
    hl_ref[0] = hl
    o = h * _gelu_tanh(g_ref[0])
    o_ref[0] = _rms_rows(o, onw_ref[...]).astype(o_ref.dtype)


def _dot_nt(a, b):
    return lax.dot_general(a, b, (((1,), (1,)), ((), ())), preferred_element_type=F32)


def _top_blocks_mask(gate, n_valid):
    nblk, tq = gate.shape
    n_iota = lax.broadcasted_iota(jnp.int32, (nblk, tq), 0)
    ok = n_iota < n_valid
    g = jnp.where(ok, gate, -jnp.inf)
    sel = jnp.zeros((nblk, tq), F32)
    for _ in range(TOP_BLOCKS):
        mx = jnp.max(g, axis=0, keepdims=True)
        first = jnp.min(jnp.where(g == mx, n_iota, nblk), axis=0, keepdims=True)
        hit = n_iota == first
        sel = jnp.where(hit, 1.0, sel)
        g = jnp.where(hit, -jnp.inf, g)
    return jnp.where(jnp.logical_and(sel > 0.0, ok), 0.0, NEG)


KEY_GROUP = 4
MAX_BLOCKS = 64
AUG = 128


def _alibi_terms():
    slopes = np.exp2(-8.0 * np.arange(1, N_HEADS + 1) / N_HEADS)
    out = np.zeros((N_HEADS, AUG - MAX_BLOCKS), np.float32)
    for h in range(N_HEADS):
        for j, val in enumerate((slopes[h] * LOG2E, slopes[h] * LOG2E * BLOCK)):
            rest = float(np.float32(val))
            for i in range(3):
                term = float(np.asarray(rest, np.float32).astype(jnp.bfloat16).astype(np.float32))
                out[h, 3 * j + i] = term
                rest -= term
    return jnp.asarray(out.reshape(N_HEADS // 2, 2, 1, AUG - MAX_BLOCKS), dtype=BF16)


def _key_aug_columns(s):
    pos = jnp.arange(s, dtype=jnp.int32)[:, None]
    col = jnp.arange(AUG, dtype=jnp.int32)[None, :]
    blk, off = pos // BLOCK, pos % BLOCK
    vals = jnp.where(col < MAX_BLOCKS, (col == blk).astype(jnp.int32),
                     jnp.where(col < MAX_BLOCKS + 3, off, jnp.where(col < MAX_BLOCKS + 6, blk, 0)))
    return vals.astype(BF16)


def _moba_prompt_kernel(q_ref, ka_ref, vt_ref, km_ref, coef_ref, o_ref, qa_ref, acc_ref, s0_ref, s1_ref):
    qi = pl.program_id(2)
    tq = q_ref.shape[1]
    nblk = km_ref.shape[1]
    t0 = pl.multiple_of(qi * tq, BLOCK)
    q = q_ref[0]
    km_hi, km_lo = _split_bf16(km_ref[0])
    lane = lax.broadcasted_iota(jnp.int32, q.shape, 1)
    key_i = lax.broadcasted_iota(jnp.int32, (BLOCK, tq), 0)
    qry_i = lax.broadcasted_iota(jnp.int32, (BLOCK, tq), 1)
    causal = key_i <= qry_i
    kd = ka_ref[0, 0, pl.ds(t0, BLOCK), :]
    nomask = jnp.zeros((tq, MAX_BLOCKS), BF16)

    init = []
    for e in range(2):
        coef = jnp.broadcast_to(coef_ref[0, e], (tq, AUG - MAX_BLOCKS))
        qh = jnp.where((lane // HEAD_DIM) == e, q, 0.0)
        q_hi, q_lo = _split_bf16(qh)
        gate = _dot_nt(km_hi, q_hi) + _dot_nt(km_hi, q_lo) + _dot_nt(km_lo, q_hi)
        mask_t = _top_blocks_mask(gate, qi)
        mask = jnp.concatenate([mask_t, jnp.zeros((128 - nblk, tq), F32)], axis=0).T[:, :MAX_BLOCKS]
        qs = (qh * (HEAD_DIM ** -0.5 * LOG2E)).astype(BF16)
        qa_ref[e * tq:(e + 1) * tq, :] = jnp.concatenate([qs, mask.astype(BF16), coef], axis=1)
        qd = jnp.concatenate([qs, nomask, coef], axis=1)
        s = jnp.where(causal, _dot_nt(kd, qd), NEG)
        m = jnp.max(s, axis=0, keepdims=True)
        p = jnp.exp2(s - m)
        vt = vt_ref[0, HEAD_DIM * e:HEAD_DIM * (e + 1), pl.ds(t0, BLOCK)]
        acc_ref[e] = jnp.dot(vt, p.astype(BF16), preferred_element_type=F32)
        init += [m, jnp.sum(p, axis=0, keepdims=True)]

    gk = KEY_GROUP * BLOCK

    def scores(g, s_ref):
        base = pl.multiple_of(g * gk, gk)
        s = _dot_nt(ka_ref[0, 0, pl.ds(base, gk), :], qa_ref[...])
        s_ref[...] = s
        return jnp.max(s, axis=0, keepdims=True)

    def softmax_pv(g, s_ref, m_old, l_old, cmax):
        base = pl.multiple_of(g * gk, gk)
        m_new = jnp.maximum(m_old, cmax)
        alpha = jnp.exp2(m_old - m_new)
        p = jnp.exp2(s_ref[...] - m_new)
        pb = p.astype(BF16)
        for e in range(2):
            vt = vt_ref[0, HEAD_DIM * e:HEAD_DIM * (e + 1), pl.ds(base, gk)]
            acc_ref[e] = (alpha[:, e * tq:(e + 1) * tq] * acc_ref[e]
                          + jnp.dot(vt, pb[:, e * tq:(e + 1) * tq], preferred_element_type=F32))
        return m_new, alpha * l_old + jnp.sum(p, axis=0, keepdims=True)

    n_groups = lax.shift_right_logical(qi + (KEY_GROUP - 1), KEY_GROUP.bit_length() - 1)
    n_pairs = lax.shift_right_logical(n_groups + 1, 1)
    last_group = nblk // KEY_GROUP - 1

    def body(gg, carry):
        m_old, l_old, cmax0 = carry
        g0 = 2 * gg
        cmax1 = scores(g0 + 1, s1_ref)
        m_mid, l_mid = softmax_pv(g0, s0_ref, m_old, l_old, cmax0)
        cmax2 = scores(jnp.minimum(g0 + 2, last_group), s0_ref)
        m_new, l_new = softmax_pv(g0 + 1, s1_ref, m_mid, l_mid, cmax1)
        return m_new, l_new, cmax2

    m0 = jnp.concatenate(init[0::2], axis=1)
    l0 = jnp.concatenate(init[1::2], axis=1)

    @pl.when(qi > 0)
    def _():
        cmax0 = scores(0, s0_ref)
        _, l_fin, _ = lax.fori_loop(0, n_pairs, body, (m0, l0, cmax0))
        o_t = jnp.concatenate([acc_ref[e] / l_fin[:, e * tq:(e + 1) * tq] for e in range(2)], axis=0)
        o_ref[0] = o_t.T

    @pl.when(qi == 0)
    def _():
        o_t = jnp.concatenate([acc_ref[e] / l0[:, e * tq:(e + 1) * tq] for e in range(2)], axis=0)
        o_ref[0] = o_t.T


def moba_prompt(q, kb, vt, kmean):
    b, s, aw = q.shape
    nblk = s // BLOCK
    hp = aw // 128
    assert nblk <= MAX_BLOCKS and nblk % (2 * KEY_GROUP) == 0 and KEY_GROUP & (KEY_GROUP - 1) == 0
    ka = jnp.concatenate([jnp.swapaxes(kb.reshape(b, s, hp, 128), 1, 2),
                          jnp.broadcast_to(_key_aug_columns(s), (b, hp, s, AUG))], axis=-1)
    return pl.pallas_call(
        _moba_prompt_kernel,
        grid=(b, hp, nblk),
        in_specs=[pl.BlockSpec((1, BLOCK, 128), lambda i, h, j: (i, j, h)),
                  pl.BlockSpec((1, 1, s, 128 + AUG), lambda i, h, j: (i, h, 0, 0)),
                  pl.BlockSpec((1, 128, s), lambda i, h, j: (i, h, 0)),
                  pl.BlockSpec((1, nblk, 128), lambda i, h, j: (i, 0, h)),
                  pl.BlockSpec((1, 2, 1, AUG - MAX_BLOCKS), lambda i, h, j: (h, 0, 0, 0))],
        out_specs=pl.BlockSpec((1, BLOCK, 128), lambda i, h, j: (i, j, h)),
        out_shape=jax.ShapeDtypeStruct((b, s, aw), F32),
        scratch_shapes=[pltpu.VMEM((2 * BLOCK, 128 + AUG), BF16), pltpu.VMEM((2, HEAD_DIM, BLOCK), F32),
                        pltpu.VMEM((KEY_GROUP * BLOCK, 2 * BLOCK), F32),
                        pltpu.VMEM((KEY_GROUP * BLOCK, 2 * BLOCK), F32)],
        compiler_params=_cparams(("arbitrary", "arbitrary", "arbitrary")),
        name="moba_prompt",
    )(q, ka, vt, kmean, _alibi_terms())


ROUTE_LANES = 128


def _route(logits):
    lane = lax.broadcasted_iota(jnp.int32, logits.shape, 1)
    ninf = -jnp.inf
    gl = jnp.where(lane < N_GROUPS, logits, ninf)
    gmax = jnp.max(gl, axis=-1, keepdims=True)
    gsel = jnp.min(jnp.where(gl == gmax, lane, ROUTE_LANES), axis=-1, keepdims=True)
    p_sel = 1.0 / jnp.sum(jnp.exp(gl - gmax), axis=-1, keepdims=True)
    lo = N_GROUPS + EXPERTS_PER_GROUP * gsel
    emask = jnp.logical_and(lane >= lo, lane < lo + EXPERTS_PER_GROUP)
    el = jnp.where(emask, logits, ninf)
    emax = jnp.max(el, axis=-1, keepdims=True)
    pe_un = jnp.exp(el - emax)
    pe = jnp.where(emask, pe_un / jnp.sum(pe_un, axis=-1, keepdims=True), -1.0)
    w1 = jnp.max(pe, axis=-1, keepdims=True)
    l1 = jnp.min(jnp.where(pe == w1, lane, ROUTE_LANES), axis=-1, keepdims=True)
    pe2 = jnp.where(lane == l1, -1.0, pe)
    w2 = jnp.max(pe2, axis=-1, keepdims=True)
    l2 = jnp.min(jnp.where(pe2 == w2, lane, ROUTE_LANES), axis=-1, keepdims=True)
    tot = w1 + w2
    out = jnp.where(lane == 0, (l1 - N_GROUPS).astype(F32), 0.0)
    out = jnp.where(lane == 1, (l2 - N_GROUPS).astype(F32), out)
    out = jnp.where(lane == 2, p_sel * (w1 / tot), out)
    out = jnp.where(lane == 3, p_sel * (w2 / tot), out)
    return out


def _outproj_kernel(oa_ref, or_ref, x_ref, anw_ref, wo_ref, fnw_ref, rwh_ref, rwl_ref, rb_ref,
                    xm_ref, xn_ref, rt_ref):
    a_n = _rms_rows(oa_ref[...], anw_ref[...]).astype(BF16)
    mix = (jnp.dot(a_n, wo_ref[0:ATTN_WIDTH, :], preferred_element_type=F32)
           + jnp.dot(or_ref[...], wo_ref[ATTN_WIDTH:, :], preferred_element_type=F32))
    xm = x_ref[...] + mix
    xm_ref[...] = xm
    xn = _rms_rows(xm, fnw_ref[...])
    xn_ref[...] = xn
    hi, lo = _split_bf16(xn)
    logits = (jnp.dot(hi, rwh_ref[...], preferred_element_type=F32)
              + jnp.dot(lo, rwh_ref[...], preferred_element_type=F32)
              + jnp.dot(hi, rwl_ref[...], preferred_element_type=F32)) + rb_ref[...]
    rt_ref[...] = _route(logits)


def out_proj(o_attn, o_rnn_n, x2d, attn_out_norm, w_out_bf, norm_ffn, rgw, rgb, rew, reb, tm):
    n = x2d.shape[0]
    assert n % tm == 0
    rw = jnp.zeros((D_MODEL, ROUTE_LANES), F32).at[:, :N_GROUPS].set(rgw).at[:, N_GROUPS:N_GROUPS + N_EXPERTS].set(rew)
    rb = jnp.zeros((1, ROUTE_LANES), F32).at[0, :N_GROUPS].set(rgb).at[0, N_GROUPS:N_GROUPS + N_EXPERTS].set(reb)
    rwh, rwl = _split_bf16(rw)
    row = lambda w: pl.BlockSpec((tm, w), lambda i: (i, 0))
    const = lambda shp: pl.BlockSpec(shp, lambda i: (0, 0))
    return pl.pallas_call(
        _outproj_kernel,
        grid=(n // tm,),
        in_specs=[row(ATTN_WIDTH), row(RNN_WIDTH), row(D_MODEL), const((1, ATTN_WIDTH)),
                  const((D_MODEL, D_MODEL)), const((1, D_MODEL)), const((D_MODEL, ROUTE_LANES)),
                  const((D_MODEL, ROUTE_LANES)), const((1, ROUTE_LANES))],
        out_specs=[row(D_MODEL), row(D_MODEL), row(ROUTE_LANES)],
        out_shape=[jax.ShapeDtypeStruct((n, D_MODEL), F32), jax.ShapeDtypeStruct((n, D_MODEL), F32),
                   jax.ShapeDtypeStruct((n, ROUTE_LANES), F32)],
        compiler_params=_cparams(("arbitrary",)),
        name="out_proj",
    )(o_attn, o_rnn_n, x2d, attn_out_norm.reshape(1, ATTN_WIDTH), w_out_bf, norm_ffn.reshape(1, D_MODEL),
      rwh, rwl, rb)


MOE_ROWS = 256


def _gather_rows(src_hbm, dst, sem, idx_ref, base, count):
    def issue(r, c):
        pltpu.make_async_copy(src_hbm.at[pl.ds(idx_ref[base + r], 1), :], dst.at[pl.ds(r, 1), :], sem).start()
        return c
    lax.fori_loop(0, count, issue, 0, unroll=8)


def _wait_rows(src_hbm, dst, sem, count):
    pltpu.make_async_copy(src_hbm.at[pl.ds(0, count), :], dst, sem).wait()


def _expert_kernel(be_ref, nu_ref, tok_ref, x_hbm, wg_ref, wu_ref, wd_ref, y_ref,
                   xbuf, sems, wg_bf, wu_bf, wd_bf):
    i = pl.program_id(0)
    n_used = nu_ref[0]
    slot = lax.rem(i, 2)

    @pl.when(i == 0)
    def _():
        _gather_rows(x_hbm, xbuf.at[0], sems.at[0], tok_ref, 0, MOE_ROWS)

    @pl.when(i + 1 < n_used)
    def _():
        _gather_rows(x_hbm, xbuf.at[1 - slot], sems.at[1 - slot], tok_ref, (i + 1) * MOE_ROWS, MOE_ROWS)

    changed = jnp.logical_or(i == 0, be_ref[i] != be_ref[jnp.maximum(i - 1, 0)])

    @pl.when(jnp.logical_and(changed, i < n_used))
    def _():
        wg_bf[...] = wg_ref[0].astype(BF16)
        wu_bf[...] = wu_ref[0].astype(BF16)
        wd_bf[...] = wd_ref[0].astype(BF16)

    @pl.when(i < n_used)
    def _():
        _wait_rows(x_hbm, xbuf.at[slot], sems.at[slot], MOE_ROWS)
        xb = xbuf[slot].astype(BF16)
        hg = jnp.dot(xb, wg_bf[...], preferred_element_type=F32)
        hu = jnp.dot(xb, wu_bf[...], preferred_element_type=F32)
        hdn = (hg * jax.nn.sigmoid(hg) * hu).astype(BF16)
        y_ref[...] = jnp.dot(hdn, wd_bf[...], preferred_element_type=F32)

    @pl.when(i >= n_used)
    def _():
        y_ref[...] = jnp.zeros_like(y_ref)


def expert_ffn(xn, block_expert, n_used, buf_tok, w_gate, w_up, w_down):
    nb = block_expert.shape[0]
    wmap = lambda i, be, nu, tok: (be[i], 0, 0)
    return pl.pallas_call(
        _expert_kernel,
        grid_spec=pltpu.PrefetchScalarGridSpec(
            num_scalar_prefetch=3,
            grid=(nb,),
            in_specs=[pl.BlockSpec(memory_space=pl.ANY),
                      pl.BlockSpec((1, D_MODEL, D_EXPERT), wmap),
                      pl.BlockSpec((1, D_MODEL, D_EXPERT), wmap),
                      pl.BlockSpec((1, D_EXPERT, D_MODEL), wmap)],
            out_specs=pl.BlockSpec((MOE_ROWS, D_MODEL), lambda i, be, nu, tok: (i, 0)),
            scratch_shapes=[pltpu.VMEM((2, MOE_ROWS, D_MODEL), F32), pltpu.SemaphoreType.DMA((2,)),
                            pltpu.VMEM((D_MODEL, D_EXPERT), BF16), pltpu.VMEM((D_MODEL, D_EXPERT), BF16),
                            pltpu.VMEM((D_EXPERT, D_MODEL), BF16)]),
        out_shape=jax.ShapeDtypeStruct((nb * MOE_ROWS, D_MODEL), F32),
        compiler_params=_cparams(("arbitrary",)),
        name="expert_ffn",
    )(block_expert, n_used, buf_tok, xn, w_gate, w_up, w_down)


COMBINE_ROWS = 256


def _combine_kernel(pos_ref, y_hbm, xm_ref, rt_ref, o_ref, ybuf, sems):
    i = pl.program_id(0)
    nsteps = pl.num_programs(0)
    slot = lax.rem(i, 2)
    rows = 2 * COMBINE_ROWS

    @pl.when(i == 0)
    def _():
        _gather_rows(y_hbm, ybuf.at[0], sems.at[0], pos_ref, 0, rows)

    @pl.when(i + 1 < nsteps)
    def _():
        _gather_rows(y_hbm, ybuf.at[1 - slot], sems.at[1 - slot], pos_ref, (i + 1) * rows, rows)

    _wait_rows(y_hbm, ybuf.at[slot], sems.at[slot], rows)
    rt = rt_ref[...]
    o_ref[...] = (xm_ref[...] + (rt[:, 2:3] * ybuf[slot, 0:COMBINE_ROWS, :]
                                 + rt[:, 3:4] * ybuf[slot, COMBINE_ROWS:rows, :]))


def moe_combine(pos_flat, yb, x_mid, route):
    n = x_mid.shape[0]
    assert n % COMBINE_ROWS == 0
    row = lambda w: pl.BlockSpec((COMBINE_ROWS, w), lambda i, pos: (i, 0))
    return pl.pallas_call(
        _combine_kernel,
        grid_spec=pltpu.PrefetchScalarGridSpec(
            num_scalar_prefetch=1,
            grid=(n // COMBINE_ROWS,),
            in_specs=[pl.BlockSpec(memory_space=pl.ANY), row(D_MODEL), row(ROUTE_LANES)],
            out_specs=row(D_MODEL),
            scratch_shapes=[pltpu.VMEM((2, 2 * COMBINE_ROWS, D_MODEL), F32), pltpu.SemaphoreType.DMA((2,))]),
        out_shape=jax.ShapeDtypeStruct((n, D_MODEL), F32),
        compiler_params=_cparams(("arbitrary",)),
        name="moe_combine",
    )(pos_flat, yb, x_mid, route)


def _moe_plan(route, n):
    a = 2 * n
    flat_e = route[:, 0:2].astype(jnp.int32).reshape(a)
    onehot = (flat_e[:, None] == jnp.arange(N_EXPERTS, dtype=jnp.int32)[None, :]).astype(jnp.int32)
    csum = jnp.cumsum(onehot, axis=0)
    counts = csum[-1]
    rank = jnp.sum(onehot * csum, axis=1) - 1
    padded = (counts + MOE_ROWS - 1) // MOE_ROWS * MOE_ROWS
    pad_end = jnp.cumsum(padded)
    pad_start = pad_end - padded
    start = jnp.cumsum(counts) - counts
    pos = (pad_start[flat_e] + rank).astype(jnp.int32)
    nb = -(-a // MOE_ROWS) + N_EXPERTS
    block_expert = jnp.minimum(jnp.sum((pad_end[None, :] <= (jnp.arange(nb) * MOE_ROWS)[:, None]).astype(jnp.int32),
                                       axis=1), N_EXPERTS - 1).astype(jnp.int32)
    n_used = (pad_end[-1] // MOE_ROWS).astype(jnp.int32).reshape(1)
    order = jnp.argsort(flat_e, stable=True)
    r = jnp.arange(nb * MOE_ROWS, dtype=jnp.int32)
    e_r = jnp.repeat(block_expert, MOE_ROWS)
    j = r - pad_start[e_r]
    buf_tok = jnp.where(j < counts[e_r], order[jnp.clip(start[e_r] + j, 0, a - 1)] // 2, 0).astype(jnp.int32)
    pos_flat = jnp.swapaxes(pos.reshape(n // COMBINE_ROWS, COMBINE_ROWS, 2), 1, 2).reshape(a)
    return block_expert, n_used, buf_tok, pos_flat


def moe_layer(x_mid, xn, route, w_gate, w_up, w_down):
    n = x_mid.shape[0]
    block_expert, n_used, buf_tok, pos_flat = _moe_plan(route, n)
    yb = expert_ffn(xn, block_expert, n_used, buf_tok, w_gate, w_up, w_down)
    return moe_combine(pos_flat, yb, x_mid, route)


PAGE_CHUNK = 16
LOCAL_ROWS = 8


def _moba_sample_kernel(pt_ref, q_ref, kn_ref, vn_ref, ck_hbm, cv_hbm, o_ref,
                        buf, sems, s_ref, p_ref, acc_ref, *, n_pages, page_rows):
    b = pl.program_id(0)
    nb = pl.num_programs(0)
    nh = N_HEADS
    t_len = q_ref.shape[1]
    aw = q_ref.shape[2]
    rows = t_len * nh
    n_chunks = n_pages // PAGE_CHUNK
    ppb = BLOCK // page_rows
    n_blocks = n_pages // ppb
    past = n_pages * page_rows

    def issue(cache, seq, chunk, slot):
        for j in range(PAGE_CHUNK):
            pg = pt_ref[seq * n_pages + chunk * PAGE_CHUNK + j]
            pltpu.make_async_copy(cache.at[0, pg], buf.at[slot, j], sems.at[slot]).start()

    def wait(cache, slot):
        pltpu.make_async_copy(cache.at[0, pl.ds(0, PAGE_CHUNK)], buf.at[slot], sems.at[slot]).wait()

    @pl.when(b == 0)
    def _():
        issue(ck_hbm, 0, 0, 0)

    q = q_ref[0]
    r_i = lax.broadcasted_iota(jnp.int32, (rows, aw), 0)
    c_i = lax.broadcasted_iota(jnp.int32, (rows, aw), 1)
    own = (c_i // HEAD_DIM) == lax.rem(r_i, nh)
    qrep = jnp.broadcast_to(q[:, None, :], (t_len, nh, aw)).reshape(rows, aw)
    qs = jnp.where(own, qrep * (HEAD_DIM ** -0.5 * LOG2E), 0.0).astype(BF16)

    def k_chunk(c, carry):
        slot = lax.rem(c, 2)

        @pl.when(c + 1 < n_chunks)
        def _():
            issue(ck_hbm, b, c + 1, 1 - slot)

        @pl.when(c + 1 == n_chunks)
        def _():
            issue(cv_hbm, b, 0, 1 - slot)

        wait(ck_hbm, slot)

        def page(j, cc):
            col0 = pl.multiple_of((c * PAGE_CHUNK + j) * page_rows, page_rows)
            kp = buf[slot, j].reshape(aw, page_rows).astype(BF16)
            s_ref[:, pl.ds(col0, page_rows)] = jnp.dot(qs, kp, preferred_element_type=F32)
            return cc
        return lax.fori_loop(0, PAGE_CHUNK, page, carry, unroll=4)

    lax.fori_loop(0, n_chunks, k_chunk, 0)

    lane = lax.broadcasted_iota(jnp.int32, (rows, 128), 1)
    gate = jnp.full((rows, 128), -jnp.inf, F32)
    for n in range(n_blocks):
        gsum = jnp.sum(s_ref[:, n * BLOCK:(n + 1) * BLOCK], axis=-1, keepdims=True)
        gate = jnp.where(lane == n, gsum, gate)
    sel = jnp.zeros((rows, 128), F32)
    for _ in range(min(TOP_BLOCKS, n_blocks)):
        mx = jnp.max(gate, axis=-1, keepdims=True)
        first = jnp.min(jnp.where(gate == mx, lane, 128), axis=-1, keepdims=True)
        hit = lane == first
        sel = jnp.where(hit, 1.0, sel)
        gate = jnp.where(hit, -jnp.inf, gate)

    r_col = lax.broadcasted_iota(jnp.int32, (rows, 1), 0)
    head_r = lax.rem(r_col, nh)
    tok_r = (r_col // nh).astype(F32)
    slope2 = lax.bitcast_convert_type(lax.shift_left(126 - head_r, 23), F32) * LOG2E
    key_in_blk = lax.broadcasted_iota(jnp.int32, (rows, BLOCK), 1).astype(F32)
    base_bias = slope2 * key_in_blk

    def blk_bias(n):
        chosen = sel[:, n:n + 1] > 0.0
        return jnp.where(chosen, slope2 * (float(n * BLOCK - past) - tok_r), NEG)

    s_loc = _dot_nt(qs, kn_ref[0].astype(BF16))
    l_col = lax.broadcasted_iota(jnp.int32, s_loc.shape, 1)
    l_tok = lax.broadcasted_iota(jnp.int32, s_loc.shape, 0) // nh
    s_loc = jnp.where(jnp.logical_and(l_col <= l_tok, l_col < t_len),
                      s_loc + slope2 * (l_col.astype(F32) - tok_r), NEG)

    m = jnp.max(s_loc, axis=-1, keepdims=True)
    for n in range(n_blocks):
        sb = s_ref[:, n * BLOCK:(n + 1) * BLOCK] + (base_bias + blk_bias(n))
        m = jnp.maximum(m, jnp.max(sb, axis=-1, keepdims=True))
    p_loc = jnp.exp2(s_loc - m)
    l = jnp.sum(p_loc, axis=-1, keepdims=True)
    for n in range(n_blocks):
        sb = s_ref[:, n * BLOCK:(n + 1) * BLOCK] + (base_bias + blk_bias(n))
        pb = jnp.exp2(sb - m)
        l = l + jnp.sum(pb, axis=-1, keepdims=True)
        p_ref[:, n * BLOCK:(n + 1) * BLOCK] = pb.astype(BF16)
    acc_ref[...] = jnp.dot(p_loc.astype(BF16), vn_ref[0].astype(BF16), preferred_element_type=F32)

    def v_chunk(c, carry):
        slot = lax.rem(c, 2)

        @pl.when(c + 1 < n_chunks)
        def _():
            issue(cv_hbm, b, c + 1, 1 - slot)

        @pl.when(jnp.logical_and(c + 1 == n_chunks, b + 1 < nb))
        def _():
            issue(ck_hbm, b + 1, 0, 1 - slot)

        wait(cv_hbm, slot)

        def page(j, cc):
            col0 = pl.multiple_of((c * PAGE_CHUNK + j) * page_rows, page_rows)
            vp = buf[slot, j].reshape(aw, page_rows).astype(BF16)
            acc_ref[...] += _dot_nt(p_ref[:, pl.ds(col0, page_rows)], vp)
            return cc
        return lax.fori_loop(0, PAGE_CHUNK, page, carry, unroll=4)

    lax.fori_loop(0, n_chunks, v_chunk, 0)
    o_full = jnp.where(own, acc_ref[...] / l, 0.0)
    o_ref[0] = jnp.sum(o_full.reshape(t_len, nh, aw), axis=1)


def moba_sample(q, k, v, cache_k, cache_v, page_table):
    db, t_len, aw = q.shape
    n_pages = page_table.shape[1]
    page_rows = cache_k.shape[2]
    assert n_pages % (2 * PAGE_CHUNK) == 0 and BLOCK % page_rows == 0 and page_rows % 128 == 0
    assert t_len <= LOCAL_ROWS and n_pages * page_rows // BLOCK <= 128
    rows = t_len * N_HEADS
    n_keys = n_pages * page_rows
    pages = lambda c: jnp.transpose(c, (0, 1, 3, 4, 2))
    padl = lambda a: jnp.pad(a, ((0, 0), (0, LOCAL_ROWS - t_len), (0, 0)))
    per_seq = lambda r: pl.BlockSpec((1, r, aw), lambda i, pt: (i, 0, 0))
    kern = functools.partial(_moba_sample_kernel, n_pages=n_pages, page_rows=page_rows)
    return pl.pallas_call(
        kern,
        grid_spec=pltpu.PrefetchScalarGridSpec(
            num_scalar_prefetch=1,
            grid=(db,),
            in_specs=[per_seq(t_len), per_seq(LOCAL_ROWS), per_seq(LOCAL_ROWS),
                      pl.BlockSpec(memory_space=pl.ANY), pl.BlockSpec(memory_space=pl.ANY)],
            out_specs=per_seq(t_len),
            scratch_shapes=[pltpu.VMEM((2, PAGE_CHUNK, N_HEADS, HEAD_DIM, page_rows), F32),
                            pltpu.SemaphoreType.DMA((2,)),
                            pltpu.VMEM((rows, n_keys), F32),
                            pltpu.VMEM((rows, n_keys), BF16),
                            pltpu.VMEM((rows, aw), F32)]),
        out_shape=jax.ShapeDtypeStruct((db, t_len, aw), F32),
        compiler_params=_cparams(("arbitrary",)),
        name="moba_sample",
    )(page_table.reshape(-1), q, padl(k), padl(v), pages(cache_k), pages(cache_v))


def _blockdiag(w):
    nb, c, d = w.shape
    eye = jnp.eye(nb, dtype=w.dtype)
    return (eye[:, None, :, None] * w[:, :, None, :]).reshape(nb * c, nb * d)


def _rnn_weights(conv_w, conv_b, lru_wa, lru_ba, lru_wx, lru_bx, lru_lambda, rnn_out_norm):
    r1 = lambda a: a.reshape(1, RNN_WIDTH)
    return (conv_w, r1(conv_b), _blockdiag(lru_wa).astype(BF16), r1(lru_ba),
            _blockdiag(lru_wx).astype(BF16), r1(lru_bx), r1(lru_lambda), r1(rnn_out_norm))


def rnn_seq(u, g, conv_buf, h0, weights, ts):
    b, s, w = u.shape
    assert s % ts == 0 and ts % 8 == 0
    cb8 = jnp.concatenate([jnp.zeros((b, 8 - (CONV_WIDTH - 1), w), F32), conv_buf], axis=1)
    tile = pl.BlockSpec((1, ts, w), lambda i, j: (i, j, 0))
    perb = lambda r: pl.BlockSpec((1, r, w), lambda i, j: (i, 0, 0))
    const = lambda shp: pl.BlockSpec(shp, lambda i, j: (0, 0))
    wspecs = [const((CONV_WIDTH, w)), const((1, w)), const((w, w)), const((1, w)), const((w, w)),
              const((1, w)), const((1, w)), const((1, w))]
    o, hl, cl = pl.pallas_call(
        _rnn_seq_kernel,
        grid=(b, s // ts),
        in_specs=[tile, tile, perb(8), perb(1)] + wspecs,
        out_specs=[tile, perb(1), perb(8)],
        out_shape=[jax.ShapeDtypeStruct((b, s, w), BF16), jax.ShapeDtypeStruct((b, 1, w), F32),
                   jax.ShapeDtypeStruct((b, 8, w), F32)],
        scratch_shapes=[pltpu.VMEM((ts + 8, w), F32), pltpu.VMEM((1, w), F32)],
        compiler_params=_cparams(("arbitrary", "arbitrary")),
        name="rnn_seq",
    )(u, g, cb8, h0.reshape(b, 1, w), *weights)
    return o, hl.reshape(b, w), cl[:, 8 - (CONV_WIDTH - 1):, :]


def _rnn_step_kernel(u_ref, g_ref, cb_ref, h0_ref, cw_ref, cbias_ref, wa_ref, ba_ref, wx_ref, bx_ref,
                     lam_ref, onw_ref, o_ref, hl_ref, cl_ref):
    t_len = u_ref.shape[0]
    taps = [cb_ref[j] for j in range(CONV_WIDTH - 1)] + [u_ref[t] for t in range(t_len)]
    h = h0_ref[...]
    for t in range(t_len):
        xc = cbias_ref[...]
        for j in range(CONV_WIDTH):
            xc = xc + cw_ref[j:j + 1, :] * taps[t + j]
        a, b = _lru_coeffs(xc, wa_ref, ba_ref, wx_ref, bx_ref, lam_ref)
        h = a * h + b
        o_ref[t] = _rms_rows(h * _gelu_tanh(g_ref[t]), onw_ref[...]).astype(o_ref.dtype)
    hl_ref[...] = h
    for j in range(CONV_WIDTH - 1):
        cl_ref[j] = taps[t_len + j]


def rnn_step(u, g, conv_buf, h0, weights):
    db, t_len, w = u.shape
    tm = lambda a: jnp.swapaxes(a, 0, 1)
    o, hl, cl = pl.pallas_call(
        _rnn_step_kernel,
        out_shape=[jax.ShapeDtypeStruct((t_len, db, w), BF16), jax.ShapeDtypeStruct((db, w), F32),
                   jax.ShapeDtypeStruct((CONV_WIDTH - 1, db, w), F32)],
        compiler_params=pltpu.CompilerParams(vmem_limit_bytes=VMEM_LIMIT),
        name="rnn_step",
    )(tm(u), tm(g), tm(conv_buf), h0, *weights)
    return tm(o), hl, tm(cl)


ROW_TILE = 512
SCAN_TILE = 256


def _trunk_layer(x, attn_fn, rnn_fn, lp):
    b, t, d = x.shape
    n = b * t
    x2d = x.reshape(n, d)
    q, k, v, kb, vb, u, g, kmean = in_proj(x2d, lp["norm_mix"], lp["w_in"], lp["q_norm"], lp["k_norm"],
                                           min(ROW_TILE, n))
    sh = lambda a: a.reshape(b, t, -1)
    o_attn = attn_fn(sh(q), sh(k), sh(v), sh(kb), sh(vb), kmean)
    o_rnn, h_last, conv_last = rnn_fn(sh(u), sh(g))
    x_mid, xn, route = out_proj(o_attn.reshape(n, -1), o_rnn.reshape(n, -1), x2d, lp["attn_out_norm"],
                                lp["w_out"], lp["norm_ffn"], lp["rgw"], lp["rgb"], lp["rew"], lp["reb"],
                                min(ROW_TILE, n))
    y = moe_layer(x_mid, xn, route, lp["egate"], lp["eup"], lp["edown"])
    heads = lambda a: a.reshape(b, t, N_HEADS, HEAD_DIM)
    return y.reshape(b, t, d), heads(k), heads(v), conv_last, h_last


def kernel(x_prompt, x_sample, cache_k, cache_v, state_conv, state_h, page_table, norm_mix, w_in, q_norm, k_norm, conv_w, conv_b, lru_wa, lru_ba, lru_wx, lru_bx, lru_lambda, attn_out_norm, rnn_out_norm, w_out, norm_ffn, router_group_w, router_group_b, router_expert_w, router_expert_b, expert_w_gate, expert_w_up, expert_w_down):
    depth = w_in.shape[0]
    yp, ys = x_prompt, x_sample
    outs = [[] for _ in range(8)]
    for l in range(depth):
        lp = dict(norm_mix=norm_mix[l], w_in=w_in[l].astype(BF16), q_norm=q_norm[l], k_norm=k_norm[l],
                  attn_out_norm=attn_out_norm[l], w_out=w_out[l].astype(BF16), norm_ffn=norm_ffn[l],
                  rgw=router_group_w[l], rgb=router_group_b[l], rew=router_expert_w[l], reb=router_expert_b[l],
                  egate=expert_w_gate[l], eup=expert_w_up[l], edown=expert_w_down[l])
        rnn_w = _rnn_weights(conv_w[l], conv_b[l], lru_wa[l], lru_ba[l], lru_wx[l], lru_bx[l],
                             lru_lambda[l], rnn_out_norm[l])
        bp = yp.shape[0]

        def prompt_attn(q, k, v, kb, vb, kmean):
            return moba_prompt(q, kb, jnp.swapaxes(vb, 1, 2), kmean.reshape(bp, -1, ATTN_WIDTH))

        def prompt_rnn(u, g):
            return rnn_seq(u, g, jnp.zeros((bp, CONV_WIDTH - 1, RNN_WIDTH), F32),
                           jnp.zeros((bp, RNN_WIDTH), F32), rnn_w, SCAN_TILE)

        def sample_attn(q, k, v, kb, vb, kmean, l=l):
            return moba_sample(q, k, v, cache_k[l:l + 1], cache_v[l:l + 1], page_table)

        def sample_rnn(u, g, l=l):
            return rnn_step(u, g, state_conv[l], state_h[l], rnn_w)

        yp, kp, vp, cp, hp = _trunk_layer(yp, prompt_attn, prompt_rnn, lp)
        ys, ks_, vs_, cs, hs = _trunk_layer(ys, sample_attn, sample_rnn, lp)
        for lst, val in zip(outs, (kp, vp, cp, hp, ks_, vs_, cs, hs)):
            lst.append(val)
    return (yp, ys) + tuple(jnp.stack(lst) for lst in outs)
```

```python
import functools

import jax
import jax.numpy as jnp
import numpy as np
from jax import lax
from jax.experimental import pallas as pl
from jax.experimental.pallas import tpu as pltpu

F32 = jnp.float32
BF16 = jnp.bfloat16

D_MODEL = 1024
ATTN_WIDTH = 512
RNN_WIDTH = 512
HEAD_DIM = 64
N_HEADS = 8
LRU_BLOCKS = 8
CONV_WIDTH = 4
LRU_C = 8.0
BLOCK = 256
TOP_BLOCKS = 3
N_GROUPS = 4
EXPERTS_PER_GROUP = 8
N_EXPERTS = 32
D_EXPERT = 512
EPS = 1e-6
D_IN = ATTN_WIDTH + 2 * ATTN_WIDTH + 2 * RNN_WIDTH
NEG = -1e30
LOG2E = 1.4426950408889634

VMEM_LIMIT = 56 * 1024 * 1024


def _cparams(sem):
    return pltpu.CompilerParams(dimension_semantics=sem, vmem_limit_bytes=VMEM_LIMIT)


def _split_bf16(a):
    hi = a.astype(BF16)
    lo = (a - hi.astype(F32)).astype(BF16)
    return hi, lo


def _inproj_kernel(x_ref, nw_ref, w_ref, qn_ref, kn_ref, gm_ref,
                   q_ref, k_ref, v_ref, kb_ref, vb_ref, u_ref, g_ref, km_ref):
    x = x_ref[...]
    ms = jnp.mean(x * x, axis=-1, keepdims=True)
    xn = x * lax.rsqrt(ms + EPS) * nw_ref[...]
    z = jnp.dot(xn.astype(BF16), w_ref[...], preferred_element_type=F32)
    aw = ATTN_WIDTH

    def head_norm(t, gain):
        hi, lo = _split_bf16(t * t)
        hm = (jnp.dot(hi, gm_ref[...], preferred_element_type=F32)
              + jnp.dot(lo, gm_ref[...], preferred_element_type=F32))
        return t * lax.rsqrt(hm + EPS) * gain

    q = head_norm(z[:, 0:aw], qn_ref[...])
    k = head_norm(z[:, aw:2 * aw], kn_ref[...])
    v = z[:, 2 * aw:3 * aw]
    q_ref[...] = q
    k_ref[...] = k
    v_ref[...] = v
    kb_ref[...] = k.astype(BF16)
    vb_ref[...] = v.astype(BF16)
    u_ref[...] = z[:, 3 * aw:3 * aw + RNN_WIDTH]
    g_ref[...] = z[:, 3 * aw + RNN_WIDTH:]
    tm = k.shape[0]
    km_ref[0] = jnp.mean(k.reshape(tm // BLOCK, BLOCK, aw), axis=1)


def _head_avg_matrix():
    r = jnp.arange(ATTN_WIDTH) // HEAD_DIM
    return jnp.where(r[:, None] == r[None, :], 1.0 / HEAD_DIM, 0.0).astype(BF16)


def in_proj(x2d, norm_w, w_in_bf, q_norm, k_norm, tm):
    n = x2d.shape[0]
    assert n % tm == 0 and tm % BLOCK == 0
    nt = n // tm
    row = lambda i: (i, 0)
    const = lambda i: (0, 0)
    wspec = lambda shp: pl.BlockSpec(shp, const)
    out_f = jax.ShapeDtypeStruct((n, ATTN_WIDTH), F32)
    out_b = jax.ShapeDtypeStruct((n, ATTN_WIDTH), BF16)
    ospec = pl.BlockSpec((tm, ATTN_WIDTH), row)
    return pl.pallas_call(
        _inproj_kernel,
        grid=(nt,),
        in_specs=[pl.BlockSpec((tm, D_MODEL), row), wspec((1, D_MODEL)), wspec((D_MODEL, D_IN)),
                  wspec((1, ATTN_WIDTH)), wspec((1, ATTN_WIDTH)), wspec((ATTN_WIDTH, ATTN_WIDTH))],
        out_specs=[ospec, ospec, ospec, ospec, ospec, ospec, ospec,
                   pl.BlockSpec((1, tm // BLOCK, ATTN_WIDTH), lambda i: (i, 0, 0))],
        out_shape=[out_f, out_f, out_f, out_b, out_b, out_f, out_f,
                   jax.ShapeDtypeStruct((nt, tm // BLOCK, ATTN_WIDTH), F32)],
        compiler_params=_cparams(("arbitrary",)),
        name="in_proj",
    )(x2d, norm_w.reshape(1, D_MODEL), w_in_bf,
      jnp.tile(q_norm, N_HEADS).reshape(1, ATTN_WIDTH), jnp.tile(k_norm, N_HEADS).reshape(1, ATTN_WIDTH),
      _head_avg_matrix())


def _gelu_tanh(x):
    return 0.5 * x * (1.0 + jnp.tanh(0.7978845608028654 * (x + 0.044715 * (x * x * x))))


def _softplus(z):
    return jnp.maximum(z, 0.0) + jnp.log1p(jnp.exp(-jnp.abs(z)))


def _lru_coeffs(xc, wa_ref, ba_ref, wx_ref, bx_ref, lam_ref):
    xb = xc.astype(BF16)
    r = jax.nn.sigmoid(jnp.dot(xb, wa_ref[...], preferred_element_type=F32) + ba_ref[...])
    i = jax.nn.sigmoid(jnp.dot(xb, wx_ref[...], preferred_element_type=F32) + bx_ref[...])
    log_a = -LRU_C * r * _softplus(-lam_ref[...])
    a = jnp.exp(log_a)
    b = jnp.sqrt(-jnp.tanh(log_a) * (a * a + 1.0)) * i * xc
    return a, b


def _rms_rows(x, gain):
    return x * lax.rsqrt(jnp.mean(x * x, axis=-1, keepdims=True) + EPS) * gain


def _rnn_seq_kernel(u_ref, g_ref, cb_ref, h0_ref, cw_ref, cbias_ref, wa_ref, ba_ref, wx_ref, bx_ref,
                    lam_ref, onw_ref, o_ref, hl_ref, cl_ref, pad_ref, h_ref):
    s = pl.program_id(1)
    ts = u_ref.shape[1]
    w = u_ref.shape[2]

    @pl.when(s == 0)
    def _():
        pad_ref[0:8, :] = cb_ref[0]
        h_ref[...] = h0_ref[0]

    u = u_ref[0]
    pad_ref[8:8 + ts, :] = u
    xc = cbias_ref[...] + cw_ref[3:4, :] * u
    for j in range(CONV_WIDTH - 1):
        xc = xc + cw_ref[j:j + 1, :] * pad_ref[5 + j:5 + j + ts, :]
    tail = pad_ref[ts:ts + 8, :]
    pad_ref[0:8, :] = tail
    cl_ref[0] = tail

    a, b = _lru_coeffs(xc, wa_ref, ba_ref, wx_ref, bx_ref, lam_ref)
    row = lax.broadcasted_iota(jnp.int32, (ts, w), 0)
    d = 1
    while d < ts:
        valid = row >= d
        a_sh = jnp.where(valid, pltpu.roll(a, d, axis=0), 1.0)
        b_sh = jnp.where(valid, pltpu.roll(b, d, axis=0), 0.0)
        b = a * b_sh + b
        a = a * a_sh
        d *= 2
    h = a * h_ref[...] + b
    hl = h[ts - 1:ts, :]
    h_ref[...] = hl
    hl_ref[0] = hl
    o = h * _gelu_tanh(g_ref[0])
    o_ref[0] = _rms_rows(o, onw_ref[...]).astype(o_ref.dtype)


def _dot_nt(a, b):
    return lax.dot_general(a, b, (((1,), (1,)), ((), ())), preferred_element_type=F32)


def _top_blocks_mask(gate, n_valid):
    nblk, tq = gate.shape
    n_iota = lax.broadcasted_iota(jnp.int32, (nblk, tq), 0)
    ok = n_iota < n_valid
    g = jnp.where(ok, gate, -jnp.inf)
    sel = jnp.zeros((nblk, tq), F32)
    for _ in range(TOP_BLOCKS):
        mx = jnp.max(g, axis=0, keepdims=True)
        first = jnp.min(jnp.where(g == mx, n_iota, nblk), axis=0, keepdims=True)
        hit = n_iota == first
        sel = jnp.where(hit, 1.0, sel)
        g = jnp.where(hit, -jnp.inf, g)
    return jnp.where(jnp.logical_and(sel > 0.0, ok), 0.0, NEG)


KEY_GROUP = 4
MAX_BLOCKS = 64
AUG = 128


def _alibi_terms():
    slopes = np.exp2(-8.0 * np.arange(1, N_HEADS + 1) / N_HEADS)
    out = np.zeros((N_HEADS, AUG - MAX_BLOCKS), np.float32)
    for h in range(N_HEADS):
        for j, val in enumerate((slopes[h] * LOG2E, slopes[h] * LOG2E * BLOCK)):
            rest = float(np.float32(val))
            for i in range(3):
                term = float(np.asarray(rest, np.float32).astype(jnp.bfloat16).astype(np.float32))
                out[h, 3 * j + i] = term
                rest -= term
    return jnp.asarray(out.reshape(N_HEADS // 2, 2, 1, AUG - MAX_BLOCKS), dtype=BF16)


def _key_aug_columns(s):
    pos = jnp.arange(s, dtype=jnp.int32)[:, None]
    col = jnp.arange(AUG, dtype=jnp.int32)[None, :]
    blk, off = pos // BLOCK, pos % BLOCK
    vals = jnp.where(col < MAX_BLOCKS, (col == blk).astype(jnp.int32),
                     jnp.where(col < MAX_BLOCKS + 3, off, jnp.where(col < MAX_BLOCKS + 6, blk, 0)))
    return vals.astype(BF16)


def _moba_prompt_kernel(q_ref, ka_ref, vt_ref, km_ref, coef_ref, o_ref, qa_ref, acc_ref, s0_ref, s1_ref):
    qi = pl.program_id(2)
    tq = q_ref.shape[1]
    nblk = km_ref.shape[1]
    t0 = pl.multiple_of(qi * tq, BLOCK)
    q = q_ref[0]
    km_hi, km_lo = _split_bf16(km_ref[0])
    lane = lax.broadcasted_iota(jnp.int32, q.shape, 1)
    key_i = lax.broadcasted_iota(jnp.int32, (BLOCK, tq), 0)
    qry_i = lax.broadcasted_iota(jnp.int32, (BLOCK, tq), 1)
    causal = key_i <= qry_i
    kd = ka_ref[0, 0, pl.ds(t0, BLOCK), :]
    nomask = jnp.zeros((tq, MAX_BLOCKS), BF16)

    init = []
    for e in range(2):
        coef = jnp.broadcast_to(coef_ref[0, e], (tq, AUG - MAX_BLOCKS))
        qh = jnp.where((lane // HEAD_DIM) == e, q, 0.0)
        q_hi, q_lo = _split_bf16(qh)
        gate = _dot_nt(km_hi, q_hi) + _dot_nt(km_hi, q_lo) + _dot_nt(km_lo, q_hi)
        mask_t = _top_blocks_mask(gate, qi)
        mask = jnp.concatenate([mask_t, jnp.zeros((128 - nblk, tq), F32)], axis=0).T[:, :MAX_BLOCKS]
        qs = (qh * (HEAD_DIM ** -0.5 * LOG2E)).astype(BF16)
        qa_ref[e * tq:(e + 1) * tq, :] = jnp.concatenate([qs, mask.astype(BF16), coef], axis=1)
        qd = jnp.concatenate([qs, nomask, coef], axis=1)
        s = jnp.where(causal, _dot_nt(kd, qd), NEG)
        m = jnp.max(s, axis=0, keepdims=True)
        p = jnp.exp2(s - m)
        vt = vt_ref[0, HEAD_DIM * e:HEAD_DIM * (e + 1), pl.ds(t0, BLOCK)]
        acc_ref[e] = jnp.dot(vt, p.astype(BF16), preferred_element_type=F32)
        init += [m, jnp.sum(p, axis=0, keepdims=True)]

    gk = KEY_GROUP * BLOCK

    def scores(g, s_ref):
        base = pl.multiple_of(g * gk, gk)
        s = _dot_nt(ka_ref[0, 0, pl.ds(base, gk), :], qa_ref[...])
        s_ref[...] = s
        return jnp.max(s, axis=0, keepdims=True)

    def softmax_pv(g, s_ref, m_old, l_old, cmax):
        base = pl.multiple_of(g * gk, gk)
        m_new = jnp.maximum(m_old, cmax)
        alpha = jnp.exp2(m_old - m_new)
        p = jnp.exp2(s_ref[...] - m_new)
        pb = p.astype(BF16)
        for e in range(2):
            vt = vt_ref[0, HEAD_DIM * e:HEAD_DIM * (e + 1), pl.ds(base, gk)]
            acc_ref[e] = (alpha[:, e * tq:(e + 1) * tq] * acc_ref[e]
                          + jnp.dot(vt, pb[:, e * tq:(e + 1) * tq], preferred_element_type=F32))
        return m_new, alpha * l_old + jnp.sum(p, axis=0, keepdims=True)

    n_groups = lax.shift_right_logical(qi + (KEY_GROUP - 1), KEY_GROUP.bit_length() - 1)
    n_pairs = lax.shift_right_logical(n_groups + 1, 1)
    last_group = nblk // KEY_GROUP - 1

    def body(gg, carry):
        m_old, l_old, cmax0 = carry
        g0 = 2 * gg
        cmax1 = scores(g0 + 1, s1_ref)
        m_mid, l_mid = softmax_pv(g0, s0_ref, m_old, l_old, cmax0)
        cmax2 = scores(jnp.minimum(g0 + 2, last_group), s0_ref)
        m_new, l_new = softmax_pv(g0 + 1, s1_ref, m_mid, l_mid, cmax1)
        return m_new, l_new, cmax2

    m0 = jnp.concatenate(init[0::2], axis=1)
    l0 = jnp.concatenate(init[1::2], axis=1)

    @pl.when(qi > 0)
    def _():
        cmax0 = scores(0, s0_ref)
        _, l_fin, _ = lax.fori_loop(0, n_pairs, body, (m0, l0, cmax0))
        o_t = jnp.concatenate([acc_ref[e] / l_fin[:, e * tq:(e + 1) * tq] for e in range(2)], axis=0)
        o_ref[0] = o_t.T

    @pl.when(qi == 0)
    def _():
        o_t = jnp.concatenate([acc_ref[e] / l0[:, e * tq:(e + 1) * tq] for e in range(2)], axis=0)
        o_ref[0] = o_t.T


def moba_prompt(q, kb, vt, kmean):
    b, s, aw = q.shape
    nblk = s // BLOCK
    hp = aw // 128
    assert nblk <= MAX_BLOCKS and nblk % (2 * KEY_GROUP) == 0 and KEY_GROUP & (KEY_GROUP - 1) == 0
    ka = jnp.concatenate([jnp.swapaxes(kb.reshape(b, s, hp, 128), 1, 2),
                          jnp.broadcast_to(_key_aug_columns(s), (b, hp, s, AUG))], axis=-1)
    return pl.pallas_call(
        _moba_prompt_kernel,
        grid=(b, hp, nblk),
        in_specs=[pl.BlockSpec((1, BLOCK, 128), lambda i, h, j: (i, j, h)),
                  pl.BlockSpec((1, 1, s, 128 + AUG), lambda i, h, j: (i, h, 0, 0)),
                  pl.BlockSpec((1, 128, s), lambda i, h, j: (i, h, 0)),
                  pl.BlockSpec((1, nblk, 128), lambda i, h, j: (i, 0, h)),
                  pl.BlockSpec((1, 2, 1, AUG - MAX_BLOCKS), lambda i, h, j: (h, 0, 0, 0))],
        out_specs=pl.BlockSpec((1, BLOCK, 128), lambda i, h, j: (i, j, h)),
        out_shape=jax.ShapeDtypeStruct((b, s, aw), F32),
        scratch_shapes=[pltpu.VMEM((2 * BLOCK, 128 + AUG), BF16), pltpu.VMEM((2, HEAD_DIM, BLOCK), F32),
                        pltpu.VMEM((KEY_GROUP * BLOCK, 2 * BLOCK), F32),
                        pltpu.VMEM((KEY_GROUP * BLOCK, 2 * BLOCK), F32)],
        compiler_params=_cparams(("arbitrary", "arbitrary", "arbitrary")),
        name="moba_prompt",
    )(q, ka, vt, kmean, _alibi_terms())


ROUTE_LANES = 128


def _route(logits):
    lane = lax.broadcasted_iota(jnp.int32, logits.shape, 1)
    ninf = -jnp.inf
    gl = jnp.where(lane < N_GROUPS, logits, ninf)
    gmax = jnp.max(gl, axis=-1, keepdims=True)
    gsel = jnp.min(jnp.where(gl == gmax, lane, ROUTE_LANES), axis=-1, keepdims=True)
    p_sel = 1.0 / jnp.sum(jnp.exp(gl - gmax), axis=-1, keepdims=True)
    lo = N_GROUPS + EXPERTS_PER_GROUP * gsel
    emask = jnp.logical_and(lane >= lo, lane < lo + EXPERTS_PER_GROUP)
    el = jnp.where(emask, logits, ninf)
    emax = jnp.max(el, axis=-1, keepdims=True)
    pe_un = jnp.exp(el - emax)
    pe = jnp.where(emask, pe_un / jnp.sum(pe_un, axis=-1, keepdims=True), -1.0)
    w1 = jnp.max(pe, axis=-1, keepdims=True)
    l1 = jnp.min(jnp.where(pe == w1, lane, ROUTE_LANES), axis=-1, keepdims=True)
    pe2 = jnp.where(lane == l1, -1.0, pe)
    w2 = jnp.max(pe2, axis=-1, keepdims=True)
    l2 = jnp.min(jnp.where(pe2 == w2, lane, ROUTE_LANES), axis=-1, keepdims=True)
    tot = w1 + w2
    out = jnp.where(lane == 0, (l1 - N_GROUPS).astype(F32), 0.0)
    out = jnp.where(lane == 1, (l2 - N_GROUPS).astype(F32), out)
    out = jnp.where(lane == 2, p_sel * (w1 / tot), out)
    out = jnp.where(lane == 3, p_sel * (w2 / tot), out)
    return out


def _outproj_kernel(oa_ref, or_ref, x_ref, anw_ref, wo_ref, fnw_ref, rwh_ref, rwl_ref, rb_ref,
                    xm_ref, xn_ref, rt_ref):
    a_n = _rms_rows(oa_ref[...], anw_ref[...]).astype(BF16)
    mix = (jnp.dot(a_n, wo_ref[0:ATTN_WIDTH, :], preferred_element_type=F32)
           + jnp.dot(or_ref[...], wo_ref[ATTN_WIDTH:, :], preferred_element_type=F32))
    xm = x_ref[...] + mix
    xm_ref[...] = xm
    xn = _rms_rows(xm, fnw_ref[...])
    xn_ref[...] = xn
    hi, lo = _split_bf16(xn)
    logits = (jnp.dot(hi, rwh_ref[...], preferred_element_type=F32)
              + jnp.dot(lo, rwh_ref[...], preferred_element_type=F32)
              + jnp.dot(hi, rwl_ref[...], preferred_element_type=F32)) + rb_ref[...]
    rt_ref[...] = _route(logits)


def _outproj_streams_kernel(*refs, n_first):
    first, second, rest = refs[0:3], refs[3:6], refs[6:]
    i = pl.program_id(0)

    @pl.when(i < n_first)
    def _():
        _outproj_kernel(*first, *rest)

    @pl.when(i >= n_first)
    def _():
        _outproj_kernel(*second, *rest)


def out_proj(stream_a, stream_b, attn_out_norm, w_out_bf, norm_ffn, rgw, rgb, rew, reb, tm):
    n_a, n_b = stream_a[2].shape[0], stream_b[2].shape[0]
    assert n_a % tm == 0 and n_b % tm == 0
    na, nb = n_a // tm, n_b // tm
    total = n_a + n_b
    rw = jnp.zeros((D_MODEL, ROUTE_LANES), F32).at[:, :N_GROUPS].set(rgw).at[:, N_GROUPS:N_GROUPS + N_EXPERTS].set(rew)
    rb = jnp.zeros((1, ROUTE_LANES), F32).at[0, :N_GROUPS].set(rgb).at[0, N_GROUPS:N_GROUPS + N_EXPERTS].set(reb)
    rwh, rwl = _split_bf16(rw)
    row_a = lambda w: pl.BlockSpec((tm, w), lambda i: (jnp.minimum(i, na - 1), 0))
    row_b = lambda w: pl.BlockSpec((tm, w), lambda i: (jnp.maximum(i - na, 0), 0))
    orow = lambda w: pl.BlockSpec((tm, w), lambda i: (i, 0))
    const = lambda shp: pl.BlockSpec(shp, lambda i: (0, 0))
    return pl.pallas_call(
        functools.partial(_outproj_streams_kernel, n_first=na),
        grid=(na + nb,),
        in_specs=[row_a(ATTN_WIDTH), row_a(RNN_WIDTH), row_a(D_MODEL), row_b(ATTN_WIDTH), row_b(RNN_WIDTH), row_b(D_MODEL),
                  const((1, ATTN_WIDTH)), const((D_MODEL, D_MODEL)), const((1, D_MODEL)), const((D_MODEL, ROUTE_LANES)),
                  const((D_MODEL, ROUTE_LANES)), const((1, ROUTE_LANES))],
        out_specs=[orow(D_MODEL), orow(D_MODEL), orow(ROUTE_LANES)],
        out_shape=[jax.ShapeDtypeStruct((total, D_MODEL), F32), jax.ShapeDtypeStruct((total, D_MODEL), F32),
                   jax.ShapeDtypeStruct((total, ROUTE_LANES), F32)],
        compiler_params=_cparams(("arbitrary",)),
        name="out_proj",
    )(*stream_a, *stream_b, attn_out_norm.reshape(1, ATTN_WIDTH), w_out_bf, norm_ffn.reshape(1, D_MODEL), rwh, rwl, rb)


MOE_ROWS = 256


def _gather_rows(src_hbm, dst, sem, idx_ref, base, count):
    def issue(r, c):
        pltpu.make_async_copy(src_hbm.at[pl.ds(idx_ref[base + r], 1), :], dst.at[pl.ds(r, 1), :], sem).start()
        return c
    lax.fori_loop(0, count, issue, 0, unroll=8)


def _wait_rows(src_hbm, dst, sem, count):
    pltpu.make_async_copy(src_hbm.at[pl.ds(0, count), :], dst, sem).wait()


def _expert_kernel(be_ref, nu_ref, tok_ref, x_hbm, wg_ref, wu_ref, wd_ref, y_ref,
                   xbuf, sems, xb_ref, wg_bf, wu_bf, wd_bf):
    i = pl.program_id(0)
    n_used = nu_ref[0]
    slot = lax.rem(i, 2)

    @pl.when(i == 0)
    def _():
        _gather_rows(x_hbm, xbuf.at[0], sems.at[0], tok_ref, 0, MOE_ROWS)

    changed = jnp.logical_or(i == 0, be_ref[i] != be_ref[jnp.maximum(i - 1, 0)])

    @pl.when(jnp.logical_and(changed, i < n_used))
    def _():
        wg_bf[...] = wg_ref[0].astype(BF16)
        wu_bf[...] = wu_ref[0].astype(BF16)
        wd_bf[...] = wd_ref[0].astype(BF16)

    @pl.when(i < n_used)
    def _():
        _wait_rows(x_hbm, xbuf.at[slot], sems.at[slot], MOE_ROWS)
        xb_ref[...] = xbuf[slot].astype(BF16)

    def compute():
        xb = xb_ref[...]
        hg = jnp.dot(xb, wg_bf[...], preferred_element_type=F32)
        hu = jnp.dot(xb, wu_bf[...], preferred_element_type=F32)
        hdn = (hg * jax.nn.sigmoid(hg) * hu).astype(BF16)
        y_ref[...] = jnp.dot(hdn, wd_bf[...], preferred_element_type=F32)

    @pl.when(i + 1 < n_used)
    def _():
        base = (i + 1) * MOE_ROWS
        for r in range(MOE_ROWS):
            pltpu.make_async_copy(x_hbm.at[pl.ds(tok_ref[base + r], 1), :],
                                  xbuf.at[1 - slot, pl.ds(r, 1), :], sems.at[1 - slot]).start()
        compute()

    @pl.when(i + 1 == n_used)
    def _():
        compute()

    @pl.when(i >= n_used)
    def _():
        y_ref[...] = jnp.zeros_like(y_ref)


def expert_ffn(xn, block_expert, n_used, buf_tok, w_gate, w_up, w_down):
    nb = block_expert.shape[0]
    wmap = lambda i, be, nu, tok: (be[i], 0, 0)
    return pl.pallas_call(
        _expert_kernel,
        grid_spec=pltpu.PrefetchScalarGridSpec(
            num_scalar_prefetch=3,
            grid=(nb,),
            in_specs=[pl.BlockSpec(memory_space=pl.ANY),
                      pl.BlockSpec((1, D_MODEL, D_EXPERT), wmap),
                      pl.BlockSpec((1, D_MODEL, D_EXPERT), wmap),
                      pl.BlockSpec((1, D_EXPERT, D_MODEL), wmap)],
            out_specs=pl.BlockSpec((MOE_ROWS, D_MODEL), lambda i, be, nu, tok: (i, 0)),
            scratch_shapes=[pltpu.VMEM((2, MOE_ROWS, D_MODEL), F32), pltpu.SemaphoreType.DMA((2,)),
                            pltpu.VMEM((MOE_ROWS, D_MODEL), BF16), pltpu.VMEM((D_MODEL, D_EXPERT), BF16), pltpu.VMEM((D_MODEL, D_EXPERT), BF16),
                            pltpu.VMEM((D_EXPERT, D_MODEL), BF16)]),
        out_shape=jax.ShapeDtypeStruct((nb * MOE_ROWS, D_MODEL), F32),
        compiler_params=_cparams(("arbitrary",)),
        name="expert_ffn",
    )(block_expert, n_used, buf_tok, xn, w_gate, w_up, w_down)


COMBINE_ROWS = 256


def _combine_kernel(pos_ref, y_hbm, xm_ref, rt_ref, o_ref, ybuf, sems, *, blk_off):
    i = pl.program_id(0)
    nsteps = pl.num_programs(0)
    slot = lax.rem(i, 2)
    rows = 2 * COMBINE_ROWS

    @pl.when(i == 0)
    def _():
        _gather_rows(y_hbm, ybuf.at[0], sems.at[0], pos_ref, blk_off * rows, rows)

    @pl.when(i + 1 < nsteps)
    def _():
        _gather_rows(y_hbm, ybuf.at[1 - slot], sems.at[1 - slot], pos_ref, (i + 1 + blk_off) * rows, rows)

    _wait_rows(y_hbm, ybuf.at[slot], sems.at[slot], rows)
    rt = rt_ref[...]
    o_ref[...] = (xm_ref[...] + (rt[:, 2:3] * ybuf[slot, 0:COMBINE_ROWS, :]
                                 + rt[:, 3:4] * ybuf[slot, COMBINE_ROWS:rows, :]))


def moe_combine(pos_flat, yb, x_mid, route, row_off, n):
    assert n % COMBINE_ROWS == 0 and row_off % COMBINE_ROWS == 0
    off = row_off // COMBINE_ROWS
    row = lambda w: pl.BlockSpec((COMBINE_ROWS, w), lambda i, pos: (i + off, 0))
    return pl.pallas_call(
        functools.partial(_combine_kernel, blk_off=off),
        grid_spec=pltpu.PrefetchScalarGridSpec(
            num_scalar_prefetch=1,
            grid=(n // COMBINE_ROWS,),
            in_specs=[pl.BlockSpec(memory_space=pl.ANY), row(D_MODEL), row(ROUTE_LANES)],
            out_specs=pl.BlockSpec((COMBINE_ROWS, D_MODEL), lambda i, pos: (i, 0)),
            scratch_shapes=[pltpu.VMEM((2, 2 * COMBINE_ROWS, D_MODEL), F32), pltpu.SemaphoreType.DMA((2,))]),
        out_shape=jax.ShapeDtypeStruct((n, D_MODEL), F32),
        compiler_params=_cparams(("arbitrary",)),
        name="moe_combine",
    )(pos_flat, yb, x_mid, route)


def _moe_plan(route, n):
    a = 2 * n
    flat_e = route[:, 0:2].astype(jnp.int32).reshape(a)
    onehot = (flat_e[:, None] == jnp.arange(N_EXPERTS, dtype=jnp.int32)[None, :]).astype(jnp.int32)
    csum = jnp.cumsum(onehot, axis=0)
    counts = csum[-1]
    rank = jnp.sum(onehot * csum, axis=1) - 1
    padded = (counts + MOE_ROWS - 1) // MOE_ROWS * MOE_ROWS
    pad_end = jnp.cumsum(padded)
    pad_start = pad_end - padded
    start = jnp.cumsum(counts) - counts
    pos = (pad_start[flat_e] + rank).astype(jnp.int32)
    nb = -(-a // MOE_ROWS) + N_EXPERTS
    block_expert = jnp.minimum(jnp.sum((pad_end[None, :] <= (jnp.arange(nb) * MOE_ROWS)[:, None]).astype(jnp.int32),
                                       axis=1), N_EXPERTS - 1).astype(jnp.int32)
    n_used = (pad_end[-1] // MOE_ROWS).astype(jnp.int32).reshape(1)
    order = jnp.argsort(flat_e, stable=True)
    r = jnp.arange(nb * MOE_ROWS, dtype=jnp.int32)
    e_r = jnp.repeat(block_expert, MOE_ROWS)
    j = r - pad_start[e_r]
    buf_tok = jnp.where(j < counts[e_r], order[jnp.clip(start[e_r] + j, 0, a - 1)] // 2, 0).astype(jnp.int32)
    pos_flat = jnp.swapaxes(pos.reshape(n // COMBINE_ROWS, COMBINE_ROWS, 2), 1, 2).reshape(a)
    return block_expert, n_used, buf_tok, pos_flat


def moe_layer(x_mid, xn, route, w_gate, w_up, w_down, splits):
    block_expert, n_used, buf_tok, pos_flat = _moe_plan(route, x_mid.shape[0])
    yb = expert_ffn(xn, block_expert, n_used, buf_tok, w_gate, w_up, w_down)
    return [moe_combine(pos_flat, yb, x_mid, route, off, n) for off, n in splits]


PAGE_CHUNK = 16
LOCAL_ROWS = 8


N_SLOTS = 3


def _moba_sample_kernel(pt_ref, q_ref, kn_ref, vn_ref, ck_hbm, cv_hbm, o_ref,
                        buf, sems, s_ref, p_ref, acc_ref, *, n_pages, page_rows):
    b = pl.program_id(0)
    nb = pl.num_programs(0)
    nh = N_HEADS
    t_len = q_ref.shape[1]
    aw = q_ref.shape[2]
    rows = t_len * nh
    n_chunks = n_pages // PAGE_CHUNK
    ppb = BLOCK // page_rows
    n_blocks = n_pages // ppb
    past = n_pages * page_rows
    per_seq = 2 * n_chunks
    slot0 = lax.rem(b * per_seq, N_SLOTS)

    def slot_of(g):
        return lax.rem(slot0 + g, N_SLOTS)

    def issue(seq, g):
        seq, g = (seq + 1, g - per_seq) if g >= per_seq else (seq, g)
        cache, chunk = (ck_hbm, g) if g < n_chunks else (cv_hbm, g - n_chunks)
        slot = lax.rem(lax.rem(seq * per_seq, N_SLOTS) + g, N_SLOTS)
        for j in range(PAGE_CHUNK):
            pg = pt_ref[seq * n_pages + chunk * PAGE_CHUNK + j]
            pltpu.make_async_copy(cache.at[0, pg], buf.at[slot, j], sems.at[slot]).start()

    def issue_ahead(g):
        if g + 2 < per_seq:
            issue(b, g + 2)
        else:
            @pl.when(b + 1 < nb)
            def _():
                issue(b, g + 2)

    def wait(cache, slot):
        pltpu.make_async_copy(cache.at[0, pl.ds(0, PAGE_CHUNK)], buf.at[slot], sems.at[slot]).wait()

    @pl.when(b == 0)
    def _():
        issue(0, 0)
        issue(0, 1)

    q = q_ref[0]
    r_i = lax.broadcasted_iota(jnp.int32, (rows, aw), 0)
    c_i = lax.broadcasted_iota(jnp.int32, (rows, aw), 1)
    own = (c_i // HEAD_DIM) == lax.rem(r_i, nh)
    qrep = jnp.broadcast_to(q[:, None, :], (t_len, nh, aw)).reshape(rows, aw)
    qs = jnp.where(own, qrep * (HEAD_DIM ** -0.5 * LOG2E), 0.0).astype(BF16)

    for c in range(n_chunks):
        slot = slot_of(c)
        issue_ahead(c)
        wait(ck_hbm, slot)

        def k_page(j, cc, c=c, slot=slot):
            col0 = pl.multiple_of((c * PAGE_CHUNK + j) * page_rows, page_rows)
            kp = buf[slot, j].reshape(aw, page_rows).astype(BF16)
            s_ref[:, pl.ds(col0, page_rows)] = jnp.dot(qs, kp, preferred_element_type=F32)
            return cc
        lax.fori_loop(0, PAGE_CHUNK, k_page, 0, unroll=4)

    lane = lax.broadcasted_iota(jnp.int32, (rows, 128), 1)
    gate = jnp.full((rows, 128), -jnp.inf, F32)
    for n in range(n_blocks):
        gsum = jnp.sum(s_ref[:, n * BLOCK:(n + 1) * BLOCK], axis=-1, keepdims=True)
        gate = jnp.where(lane == n, gsum, gate)
    sel = jnp.zeros((rows, 128), F32)
    for _ in range(min(TOP_BLOCKS, n_blocks)):
        mx = jnp.max(gate, axis=-1, keepdims=True)
        first = jnp.min(jnp.where(gate == mx, lane, 128), axis=-1, keepdims=True)
        hit = lane == first
        sel = jnp.where(hit, 1.0, sel)
        gate = jnp.where(hit, -jnp.inf, gate)

    r_col = lax.broadcasted_iota(jnp.int32, (rows, 1), 0)
    head_r = lax.rem(r_col, nh)
    tok_r = (r_col // nh).astype(F32)
    slope2 = lax.bitcast_convert_type(lax.shift_left(126 - head_r, 23), F32) * LOG2E
    key_in_blk = lax.broadcasted_iota(jnp.int32, (rows, BLOCK), 1).astype(F32)
    base_bias = slope2 * key_in_blk

    def biased(n):
        chosen = sel[:, n:n + 1] > 0.0
        blk = jnp.where(chosen, slope2 * (float(n * BLOCK - past) - tok_r), NEG)
        return s_ref[:, n * BLOCK:(n + 1) * BLOCK] + (base_bias + blk)

    s_loc = _dot_nt(qs, kn_ref[0].astype(BF16))
    l_col = lax.broadcasted_iota(jnp.int32, s_loc.shape, 1)
    l_tok = lax.broadcasted_iota(jnp.int32, s_loc.shape, 0) // nh
    s_loc = jnp.where(jnp.logical_and(l_col <= l_tok, l_col < t_len),
                      s_loc + slope2 * (l_col.astype(F32) - tok_r), NEG)

    m_acc = biased(0)
    for n in range(1, n_blocks):
        m_acc = jnp.maximum(m_acc, biased(n))
    m = jnp.maximum(jnp.max(m_acc, axis=-1, keepdims=True), jnp.max(s_loc, axis=-1, keepdims=True))
    p_loc = jnp.exp2(s_loc - m)
    l_acc = jnp.zeros((rows, BLOCK), F32)
    for n in range(n_blocks):
        pb = jnp.exp2(biased(n) - m)
        l_acc = l_acc + pb
        p_ref[:, n * BLOCK:(n + 1) * BLOCK] = pb.astype(BF16)
    l = jnp.sum(l_acc, axis=-1, keepdims=True) + jnp.sum(p_loc, axis=-1, keepdims=True)
    acc_ref[...] = jnp.dot(p_loc.astype(BF16), vn_ref[0].astype(BF16), preferred_element_type=F32)

    for c in range(n_chunks):
        slot = slot_of(n_chunks + c)
        issue_ahead(n_chunks + c)
        wait(cv_hbm, slot)

        def v_page(j, cc, c=c, slot=slot):
            col0 = pl.multiple_of((c * PAGE_CHUNK + j) * page_rows, page_rows)
            vp = buf[slot, j].reshape(aw, page_rows).astype(BF16)
            acc_ref[...] += _dot_nt(p_ref[:, pl.ds(col0, page_rows)], vp)
            return cc
        lax.fori_loop(0, PAGE_CHUNK, v_page, 0, unroll=4)

    o_full = jnp.where(own, acc_ref[...] / l, 0.0)
    o_ref[0] = jnp.sum(o_full.reshape(t_len, nh, aw), axis=1)


def moba_sample(q, k, v, cache_k, cache_v, page_table):
    db, t_len, aw = q.shape
    n_pages = page_table.shape[1]
    page_rows = cache_k.shape[2]
    assert n_pages % PAGE_CHUNK == 0 and BLOCK % page_rows == 0 and page_rows % 128 == 0
    assert t_len <= LOCAL_ROWS and n_pages * page_rows // BLOCK <= 128
    rows = t_len * N_HEADS
    n_keys = n_pages * page_rows
    pages = lambda c: jnp.transpose(c, (0, 1, 3, 4, 2))
    padl = lambda a: jnp.pad(a, ((0, 0), (0, LOCAL_ROWS - t_len), (0, 0)))
    per_seq = lambda r: pl.BlockSpec((1, r, aw), lambda i, pt: (i, 0, 0))
    kern = functools.partial(_moba_sample_kernel, n_pages=n_pages, page_rows=page_rows)
    return pl.pallas_call(
        kern,
        grid_spec=pltpu.PrefetchScalarGridSpec(
            num_scalar_prefetch=1,
            grid=(db,),
            in_specs=[per_seq(t_len), per_seq(LOCAL_ROWS), per_seq(LOCAL_ROWS),
                      pl.BlockSpec(memory_space=pl.ANY), pl.BlockSpec(memory_space=pl.ANY)],
            out_specs=per_seq(t_len),
            scratch_shapes=[pltpu.VMEM((N_SLOTS, PAGE_CHUNK, N_HEADS, HEAD_DIM, page_rows), F32),
                            pltpu.SemaphoreType.DMA((N_SLOTS,)),
                            pltpu.VMEM((rows, n_keys), F32),
                            pltpu.VMEM((rows, n_keys), BF16),
                            pltpu.VMEM((rows, aw), F32)]),
        out_shape=jax.ShapeDtypeStruct((db, t_len, aw), F32),
        compiler_params=_cparams(("arbitrary",)),
        name="moba_sample",
    )(page_table.reshape(-1), q, padl(k), padl(v), pages(cache_k), pages(cache_v))


def _blockdiag(w):
    nb, c, d = w.shape
    eye = jnp.eye(nb, dtype=w.dtype)
    return (eye[:, None, :, None] * w[:, :, None, :]).reshape(nb * c, nb * d)


def _rnn_weights(conv_w, conv_b, lru_wa, lru_ba, lru_wx, lru_bx, lru_lambda, rnn_out_norm):
    r1 = lambda a: a.reshape(1, RNN_WIDTH)
    return (conv_w, r1(conv_b), _blockdiag(lru_wa).astype(BF16), r1(lru_ba),
            _blockdiag(lru_wx).astype(BF16), r1(lru_bx), r1(lru_lambda), r1(rnn_out_norm))


def rnn_seq(u, g, conv_buf, h0, weights, ts):
    b, s, w = u.shape
    assert s % ts == 0 and ts % 8 == 0
    cb8 = jnp.concatenate([jnp.zeros((b, 8 - (CONV_WIDTH - 1), w), F32), conv_buf], axis=1)
    tile = pl.BlockSpec((1, ts, w), lambda i, j: (i, j, 0))
    perb = lambda r: pl.BlockSpec((1, r, w), lambda i, j: (i, 0, 0))
    const = lambda shp: pl.BlockSpec(shp, lambda i, j: (0, 0))
    wspecs = [const((CONV_WIDTH, w)), const((1, w)), const((w, w)), const((1, w)), const((w, w)),
              const((1, w)), const((1, w)), const((1, w))]
    o, hl, cl = pl.pallas_call(
        _rnn_seq_kernel,
        grid=(b, s // ts),
        in_specs=[tile, tile, perb(8), perb(1)] + wspecs,
        out_specs=[tile, perb(1), perb(8)],
        out_shape=[jax.ShapeDtypeStruct((b, s, w), BF16), jax.ShapeDtypeStruct((b, 1, w), F32),
                   jax.ShapeDtypeStruct((b, 8, w), F32)],
        scratch_shapes=[pltpu.VMEM((ts + 8, w), F32), pltpu.VMEM((1, w), F32)],
        compiler_params=_cparams(("arbitrary", "arbitrary")),
        name="rnn_seq",
    )(u, g, cb8, h0.reshape(b, 1, w), *weights)
    return o, hl.reshape(b, w), cl[:, 8 - (CONV_WIDTH - 1):, :]


def _rnn_step_kernel(u_ref, g_ref, cb_ref, h0_ref, cw_ref, cbias_ref, wa_ref, ba_ref, wx_ref, bx_ref,
                     lam_ref, onw_ref, o_ref, hl_ref, cl_ref):
    t_len = u_ref.shape[0]
    taps = [cb_ref[j] for j in range(CONV_WIDTH - 1)] + [u_ref[t] for t in range(t_len)]
    h = h0_ref[...]
    for t in range(t_len):
        xc = cbias_ref[...]
        for j in range(CONV_WIDTH):
            xc = xc + cw_ref[j:j + 1, :] * taps[t + j]
        a, b = _lru_coeffs(xc, wa_ref, ba_ref, wx_ref, bx_ref, lam_ref)
        h = a * h + b
        o_ref[t] = _rms_rows(h * _gelu_tanh(g_ref[t]), onw_ref[...]).astype(o_ref.dtype)
    hl_ref[...] = h
    for j in range(CONV_WIDTH - 1):
        cl_ref[j] = taps[t_len + j]


def rnn_step(u, g, conv_buf, h0, weights):
    db, t_len, w = u.shape
    tm = lambda a: jnp.swapaxes(a, 0, 1)
    o, hl, cl = pl.pallas_call(
        _rnn_step_kernel,
        out_shape=[jax.ShapeDtypeStruct((t_len, db, w), BF16), jax.ShapeDtypeStruct((db, w), F32),
                   jax.ShapeDtypeStruct((CONV_WIDTH - 1, db, w), F32)],
        compiler_params=pltpu.CompilerParams(vmem_limit_bytes=VMEM_LIMIT),
        name="rnn_step",
    )(tm(u), tm(g), tm(conv_buf), h0, *weights)
    return tm(o), hl, tm(cl)


ROW_TILE = 512
SCAN_TILE = 256


def _mixers(x, attn_fn, rnn_fn, lp):
    b, t, d = x.shape
    n = b * t
    q, k, v, kb, vb, u, g, kmean = in_proj(x.reshape(n, d), lp["norm_mix"], lp["w_in"], lp["q_norm"], lp["k_norm"],
                                           min(ROW_TILE, n))
    sh = lambda a: a.reshape(b, t, -1)
    o_attn = attn_fn(sh(q), sh(k), sh(v), sh(kb), sh(vb), kmean)
    o_rnn, h_last, conv_last = rnn_fn(sh(u), sh(g))
    heads = lambda a: a.reshape(b, t, N_HEADS, HEAD_DIM)
    return o_attn.reshape(n, -1), o_rnn.reshape(n, -1), heads(k), heads(v), conv_last, h_last


def kernel(x_prompt, x_sample, cache_k, cache_v, state_conv, state_h, page_table, norm_mix, w_in, q_norm, k_norm, conv_w, conv_b, lru_wa, lru_ba, lru_wx, lru_bx, lru_lambda, attn_out_norm, rnn_out_norm, w_out, norm_ffn, router_group_w, router_group_b, router_expert_w, router_expert_b, expert_w_gate, expert_w_up, expert_w_down):
    depth = w_in.shape[0]
    yp, ys = x_prompt, x_sample
    n_p = yp.shape[0] * yp.shape[1]
    n_s = ys.shape[0] * ys.shape[1]
    outs = [[] for _ in range(8)]
    for l in range(depth):
        lp = dict(norm_mix=norm_mix[l], w_in=w_in[l].astype(BF16), q_norm=q_norm[l], k_norm=k_norm[l])
        rnn_w = _rnn_weights(conv_w[l], conv_b[l], lru_wa[l], lru_ba[l], lru_wx[l], lru_bx[l],
                             lru_lambda[l], rnn_out_norm[l])
        bp = yp.shape[0]

        def prompt_attn(q, k, v, kb, vb, kmean):
            return moba_prompt(q, kb, jnp.swapaxes(vb, 1, 2), kmean.reshape(bp, -1, ATTN_WIDTH))

        def prompt_rnn(u, g):
            return rnn_seq(u, g, jnp.zeros((bp, CONV_WIDTH - 1, RNN_WIDTH), F32),
                           jnp.zeros((bp, RNN_WIDTH), F32), rnn_w, SCAN_TILE)

        def sample_attn(q, k, v, kb, vb, kmean, l=l):
            return moba_sample(q, k, v, cache_k[l:l + 1], cache_v[l:l + 1], page_table)

        def sample_rnn(u, g, l=l):
            return rnn_step(u, g, state_conv[l], state_h[l], rnn_w)

        oa_p, or_p, kp, vp, cp, hp = _mixers(yp, prompt_attn, prompt_rnn, lp)
        oa_s, or_s, ks_, vs_, cs, hs = _mixers(ys, sample_attn, sample_rnn, lp)
        x_mid, xn, route = out_proj((oa_p, or_p, yp.reshape(n_p, D_MODEL)), (oa_s, or_s, ys.reshape(n_s, D_MODEL)),
                                    attn_out_norm[l], w_out[l].astype(BF16), norm_ffn[l], router_group_w[l],
                                    router_group_b[l], router_expert_w[l], router_expert_b[l],
                                    min(ROW_TILE, n_p, n_s))
        y_p, y_s = moe_layer(x_mid, xn, route, expert_w_gate[l], expert_w_up[l], expert_w_down[l],
                             ((0, n_p), (n_p, n_s)))
        yp, ys = y_p.reshape(yp.shape), y_s.reshape(ys.shape)
        for lst, val in zip(outs, (kp, vp, cp, hp, ks_, vs_, cs, hs)):
            lst.append(val)
    return (yp, ys) + tuple(jnp.stack(lst) for lst in outs)
```

```python
import functools

import jax
import jax.numpy as jnp
import numpy as np
from jax import lax
from jax.experimental import pallas as pl
from jax.experimental.pallas import tpu as pltpu

F32 = jnp.float32
BF16 = jnp.bfloat16

D_MODEL = 1024
ATTN_WIDTH = 512
RNN_WIDTH = 512
HEAD_DIM = 64
N_HEADS = 8
LRU_BLOCKS = 8
CONV_WIDTH = 4
LRU_C = 8.0
BLOCK = 256
TOP_BLOCKS = 3
N_GROUPS = 4
EXPERTS_PER_GROUP = 8
N_EXPERTS = 32
D_EXPERT = 512
EPS = 1e-6
D_IN = ATTN_WIDTH + 2 * ATTN_WIDTH + 2 * RNN_WIDTH
NEG = -1e30
LOG2E = 1.4426950408889634

VMEM_LIMIT = 56 * 1024 * 1024


def _cparams(sem):
    return pltpu.CompilerParams(dimension_semantics=sem, vmem_limit_bytes=VMEM_LIMIT)


def _split_bf16(a):
    hi = a.astype(BF16)
    lo = (a - hi.astype(F32)).astype(BF16)
    return hi, lo


def _inproj_kernel(x_ref, nw_ref, w_ref, qn_ref, kn_ref, gm_ref,
                   q_ref, k_ref, v_ref, kb_ref, vb_ref, u_ref, g_ref, km_ref):
    x = x_ref[...]
    ms = jnp.mean(x * x, axis=-1, keepdims=True)
    xn = x * lax.rsqrt(ms + EPS) * nw_ref[...]
    z = jnp.dot(xn.astype(BF16), w_ref[...], preferred_element_type=F32)
    aw = ATTN_WIDTH

    def head_norm(t, gain):
        hi, lo = _split_bf16(t * t)
        hm = (jnp.dot(hi, gm_ref[...], preferred_element_type=F32)
              + jnp.dot(lo, gm_ref[...], preferred_element_type=F32))
        return t * lax.rsqrt(hm + EPS) * gain

    q = head_norm(z[:, 0:aw], qn_ref[...])
    k = head_norm(z[:, aw:2 * aw], kn_ref[...])
    v = z[:, 2 * aw:3 * aw]
    q_ref[...] = q
    k_ref[...] = k
    v_ref[...] = v
    kb_ref[...] = k.astype(BF16)
    vb_ref[...] = v.astype(BF16)
    u_ref[...] = z[:, 3 * aw:3 * aw + RNN_WIDTH]
    g_ref[...] = z[:, 3 * aw + RNN_WIDTH:]
    tm = k.shape[0]
    km_ref[0] = jnp.mean(k.reshape(tm // BLOCK, BLOCK, aw), axis=1)


def _head_avg_matrix():
    r = jnp.arange(ATTN_WIDTH) // HEAD_DIM
    return jnp.where(r[:, None] == r[None, :], 1.0 / HEAD_DIM, 0.0).astype(BF16)


def in_proj(x2d, norm_w, w_in_bf, q_norm, k_norm, tm):
    n = x2d.shape[0]
    assert n % tm == 0 and tm % BLOCK == 0
    nt = n // tm
    row = lambda i: (i, 0)
    const = lambda i: (0, 0)
    wspec = lambda shp: pl.BlockSpec(shp, const)
    out_f = jax.ShapeDtypeStruct((n, ATTN_WIDTH), F32)
    out_b = jax.ShapeDtypeStruct((n, ATTN_WIDTH), BF16)
    ospec = pl.BlockSpec((tm, ATTN_WIDTH), row)
    return pl.pallas_call(
        _inproj_kernel,
        grid=(nt,),
        in_specs=[pl.BlockSpec((tm, D_MODEL), row), wspec((1, D_MODEL)), wspec((D_MODEL, D_IN)),
                  wspec((1, ATTN_WIDTH)), wspec((1, ATTN_WIDTH)), wspec((ATTN_WIDTH, ATTN_WIDTH))],
        out_specs=[ospec, ospec, ospec, ospec, ospec, ospec, ospec,
                   pl.BlockSpec((1, tm // BLOCK, ATTN_WIDTH), lambda i: (i, 0, 0))],
        out_shape=[out_f, out_f, out_f, out_b, out_b, out_f, out_f,
                   jax.ShapeDtypeStruct((nt, tm // BLOCK, ATTN_WIDTH), F32)],
        compiler_params=_cparams(("arbitrary",)),
        name="in_proj",
    )(x2d, norm_w.reshape(1, D_MODEL), w_in_bf,
      jnp.tile(q_norm, N_HEADS).reshape(1, ATTN_WIDTH), jnp.tile(k_norm, N_HEADS).reshape(1, ATTN_WIDTH),
      _head_avg_matrix())


def _gelu_tanh(x):
    return 0.5 * x * (1.0 + jnp.tanh(0.7978845608028654 * (x + 0.044715 * (x * x * x))))


def _softplus(z):
    return jnp.maximum(z, 0.0) + jnp.log1p(jnp.exp(-jnp.abs(z)))


def _lru_coeffs(xc, wa_ref, ba_ref, wx_ref, bx_ref, lam_ref):
    xb = xc.astype(BF16)
    r = jax.nn.sigmoid(jnp.dot(xb, wa_ref[...], preferred_element_type=F32) + ba_ref[...])
    i = jax.nn.sigmoid(jnp.dot(xb, wx_ref[...], preferred_element_type=F32) + bx_ref[...])
    log_a = -LRU_C * r * _softplus(-lam_ref[...])
    a = jnp.exp(log_a)
    b = jnp.sqrt(-jnp.tanh(log_a) * (a * a + 1.0)) * i * xc
    return a, b


def _rms_rows(x, gain):
    return x * lax.rsqrt(jnp.mean(x * x, axis=-1, keepdims=True) + EPS) * gain


def _rnn_seq_kernel(u_ref, g_ref, cb_ref, h0_ref, cw_ref, cbias_ref, wa_ref, ba_ref, wx_ref, bx_ref,
                    lam_ref, onw_ref, o_ref, hl_ref, cl_ref, pad_ref, h_ref):
    s = pl.program_id(1)
    ts = u_ref.shape[1]
    w = u_ref.shape[2]

    @pl.when(s == 0)
    def _():
        pad_ref[0:8, :] = cb_ref[0]
        h_ref[...] = h0_ref[0]

    u = u_ref[0]
    pad_ref[8:8 + ts, :] = u
    xc = cbias_ref[...] + cw_ref[3:4, :] * u
    for j in range(CONV_WIDTH - 1):
        xc = xc + cw_ref[j:j + 1, :] * pad_ref[5 + j:5 + j + ts, :]
    tail = pad_ref[ts:ts + 8, :]
    pad_ref[0:8, :] = tail
    cl_ref[0] = tail

    a, b = _lru_coeffs(xc, wa_ref, ba_ref, wx_ref, bx_ref, lam_ref)
    row = lax.broadcasted_iota(jnp.int32, (ts, w), 0)
    d = 1
    while d < ts:
        valid = row >= d
        a_sh = jnp.where(valid, pltpu.roll(a, d, axis=0), 1.0)
        b_sh = jnp.where(valid, pltpu.roll(b, d, axis=0), 0.0)
        b = a * b_sh + b
        a = a * a_sh
        d *= 2
    h = a * h_ref[...] + b
    hl = h[ts - 1:ts, :]
    h_ref[...] = hl
    hl_ref[0] = hl
    o = h * _gelu_tanh(g_ref[0])
    o_ref[0] = _rms_rows(o, onw_ref[...]).astype(o_ref.dtype)


def _dot_nt(a, b):
    return lax.dot_general(a, b, (((1,), (1,)), ((), ())), preferred_element_type=F32)


def _top_blocks_mask(gate, n_valid):
    nblk, tq = gate.shape
    n_iota = lax.broadcasted_iota(jnp.int32, (nblk, tq), 0)
    ok = n_iota < n_valid
    g = jnp.where(ok, gate, -jnp.inf)
    sel = jnp.zeros((nblk, tq), F32)
    for _ in range(TOP_BLOCKS):
        mx = jnp.max(g, axis=0, keepdims=True)
        first = jnp.min(jnp.where(g == mx, n_iota, nblk), axis=0, keepdims=True)
        hit = n_iota == first
        sel = jnp.where(hit, 1.0, sel)
        g = jnp.where(hit, -jnp.inf, g)
    return jnp.where(jnp.logical_and(sel > 0.0, ok), 0.0, NEG)


KEY_GROUP = 4
MAX_BLOCKS = 64
AUG = 128


def _alibi_terms():
    slopes = np.exp2(-8.0 * np.arange(1, N_HEADS + 1) / N_HEADS)
    out = np.zeros((N_HEADS, AUG - MAX_BLOCKS), np.float32)
    for h in range(N_HEADS):
        for j, val in enumerate((slopes[h] * LOG2E, slopes[h] * LOG2E * BLOCK)):
            rest = float(np.float32(val))
            for i in range(3):
                term = float(np.asarray(rest, np.float32).astype(jnp.bfloat16).astype(np.float32))
                out[h, 3 * j + i] = term
                rest -= term
    return jnp.asarray(out.reshape(N_HEADS // 2, 2, 1, AUG - MAX_BLOCKS), dtype=BF16)


def _key_aug_columns(s):
    pos = jnp.arange(s, dtype=jnp.int32)[:, None]
    col = jnp.arange(AUG, dtype=jnp.int32)[None, :]
    blk, off = pos // BLOCK, pos % BLOCK
    vals = jnp.where(col < MAX_BLOCKS, (col == blk).astype(jnp.int32),
                     jnp.where(col < MAX_BLOCKS + 3, off, jnp.where(col < MAX_BLOCKS + 6, blk, 0)))
    return vals.astype(BF16)


def _moba_prompt_kernel(q_ref, ka_ref, vt_ref, km_ref, coef_ref, o_ref, qa_ref, acc_ref, s0_ref, s1_ref):
    qi = pl.program_id(2)
    tq = q_ref.shape[1]
    nblk = km_ref.shape[1]
    t0 = pl.multiple_of(qi * tq, BLOCK)
    q = q_ref[0]
    km_hi, km_lo = _split_bf16(km_ref[0])
    lane = lax.broadcasted_iota(jnp.int32, q.shape, 1)
    key_i = lax.broadcasted_iota(jnp.int32, (BLOCK, tq), 0)
    qry_i = lax.broadcasted_iota(jnp.int32, (BLOCK, tq), 1)
    causal = key_i <= qry_i
    kd = ka_ref[0, 0, pl.ds(t0, BLOCK), :]
    nomask = jnp.zeros((tq, MAX_BLOCKS), BF16)

    init = []
    for e in range(2):
        coef = jnp.broadcast_to(coef_ref[0, e], (tq, AUG - MAX_BLOCKS))
        qh = jnp.where((lane // HEAD_DIM) == e, q, 0.0)
        q_hi, q_lo = _split_bf16(qh)
        gate = _dot_nt(km_hi, q_hi) + _dot_nt(km_hi, q_lo) + _dot_nt(km_lo, q_hi)
        mask_t = _top_blocks_mask(gate, qi)
        mask = jnp.concatenate([mask_t, jnp.zeros((128 - nblk, tq), F32)], axis=0).T[:, :MAX_BLOCKS]
        qs = (qh * (HEAD_DIM ** -0.5 * LOG2E)).astype(BF16)
        qa_ref[e * tq:(e + 1) * tq, :] = jnp.concatenate([qs, mask.astype(BF16), coef], axis=1)
        qd = jnp.concatenate([qs, nomask, coef], axis=1)
        s = jnp.where(causal, _dot_nt(kd, qd), NEG)
        m = jnp.max(s, axis=0, keepdims=True)
        p = jnp.exp2(s - m)
        vt = vt_ref[0, HEAD_DIM * e:HEAD_DIM * (e + 1), pl.ds(t0, BLOCK)]
        acc_ref[e] = jnp.dot(vt, p.astype(BF16), preferred_element_type=F32)
        init += [m, jnp.sum(p, axis=0, keepdims=True)]

    gk = KEY_GROUP * BLOCK

    def scores(g, s_ref):
        base = pl.multiple_of(g * gk, gk)
        s = _dot_nt(ka_ref[0, 0, pl.ds(base, gk), :], qa_ref[...])
        s_ref[...] = s
        return jnp.max(s, axis=0, keepdims=True)

    def softmax_pv(g, s_ref, m_old, l_old, cmax):
        base = pl.multiple_of(g * gk, gk)
        m_new = jnp.maximum(m_old, cmax)
        alpha = jnp.exp2(m_old - m_new)
        p = jnp.exp2(s_ref[...] - m_new)
        pb = p.astype(BF16)
        for e in range(2):
            vt = vt_ref[0, HEAD_DIM * e:HEAD_DIM * (e + 1), pl.ds(base, gk)]
            acc_ref[e] = (alpha[:, e * tq:(e + 1) * tq] * acc_ref[e]
                          + jnp.dot(vt, pb[:, e * tq:(e + 1) * tq], preferred_element_type=F32))
        return m_new, alpha * l_old + jnp.sum(p, axis=0, keepdims=True)

    n_groups = lax.shift_right_logical(qi + (KEY_GROUP - 1), KEY_GROUP.bit_length() - 1)
    n_pairs = lax.shift_right_logical(n_groups + 1, 1)
    last_group = nblk // KEY_GROUP - 1

    def body(gg, carry):
        m_old, l_old, cmax0 = carry
        g0 = 2 * gg
        cmax1 = scores(g0 + 1, s1_ref)
        m_mid, l_mid = softmax_pv(g0, s0_ref, m_old, l_old, cmax0)
        cmax2 = scores(jnp.minimum(g0 + 2, last_group), s0_ref)
        m_new, l_new = softmax_pv(g0 + 1, s1_ref, m_mid, l_mid, cmax1)
        return m_new, l_new, cmax2

    m0 = jnp.concatenate(init[0::2], axis=1)
    l0 = jnp.concatenate(init[1::2], axis=1)

    @pl.when(qi > 0)
    def _():
        cmax0 = scores(0, s0_ref)
        _, l_fin, _ = lax.fori_loop(0, n_pairs, body, (m0, l0, cmax0))
        o_t = jnp.concatenate([acc_ref[e] / l_fin[:, e * tq:(e + 1) * tq] for e in range(2)], axis=0)
        o_ref[0] = o_t.T

    @pl.when(qi == 0)
    def _():
        o_t = jnp.concatenate([acc_ref[e] / l0[:, e * tq:(e + 1) * tq] for e in range(2)], axis=0)
        o_ref[0] = o_t.T


def moba_prompt(q, kb, vt, kmean):
    b, s, aw = q.shape
    nblk = s // BLOCK
    hp = aw // 128
    assert nblk <= MAX_BLOCKS and nblk % (2 * KEY_GROUP) == 0 and KEY_GROUP & (KEY_GROUP - 1) == 0
    ka = jnp.concatenate([jnp.swapaxes(kb.reshape(b, s, hp, 128), 1, 2),
                          jnp.broadcast_to(_key_aug_columns(s), (b, hp, s, AUG))], axis=-1)
    return pl.pallas_call(
        _moba_prompt_kernel,
        grid=(b, hp, nblk),
        in_specs=[pl.BlockSpec((1, BLOCK, 128), lambda i, h, j: (i, j, h)),
                  pl.BlockSpec((1, 1, s, 128 + AUG), lambda i, h, j: (i, h, 0, 0)),
                  pl.BlockSpec((1, 128, s), lambda i, h, j: (i, h, 0)),
                  pl.BlockSpec((1, nblk, 128), lambda i, h, j: (i, 0, h)),
                  pl.BlockSpec((1, 2, 1, AUG - MAX_BLOCKS), lambda i, h, j: (h, 0, 0, 0))],
        out_specs=pl.BlockSpec((1, BLOCK, 128), lambda i, h, j: (i, j, h)),
        out_shape=jax.ShapeDtypeStruct((b, s, aw), F32),
        scratch_shapes=[pltpu.VMEM((2 * BLOCK, 128 + AUG), BF16), pltpu.VMEM((2, HEAD_DIM, BLOCK), F32),
                        pltpu.VMEM((KEY_GROUP * BLOCK, 2 * BLOCK), F32),
                        pltpu.VMEM((KEY_GROUP * BLOCK, 2 * BLOCK), F32)],
        compiler_params=_cparams(("arbitrary", "arbitrary", "arbitrary")),
        name="moba_prompt",
    )(q, ka, vt, kmean, _alibi_terms())


ROUTE_LANES = 128


def _route(logits):
    lane = lax.broadcasted_iota(jnp.int32, logits.shape, 1)
    ninf = -jnp.inf
    gl = jnp.where(lane < N_GROUPS, logits, ninf)
    gmax = jnp.max(gl, axis=-1, keepdims=True)
    gsel = jnp.min(jnp.where(gl == gmax, lane, ROUTE_LANES), axis=-1, keepdims=True)
    p_sel = 1.0 / jnp.sum(jnp.exp(gl - gmax), axis=-1, keepdims=True)
    lo = N_GROUPS + EXPERTS_PER_GROUP * gsel
    emask = jnp.logical_and(lane >= lo, lane < lo + EXPERTS_PER_GROUP)
    el = jnp.where(emask, logits, ninf)
    emax = jnp.max(el, axis=-1, keepdims=True)
    pe_un = jnp.exp(el - emax)
    pe = jnp.where(emask, pe_un / jnp.sum(pe_un, axis=-1, keepdims=True), -1.0)
    w1 = jnp.max(pe, axis=-1, keepdims=True)
    l1 = jnp.min(jnp.where(pe == w1, lane, ROUTE_LANES), axis=-1, keepdims=True)
    pe2 = jnp.where(lane == l1, -1.0, pe)
    w2 = jnp.max(pe2, axis=-1, keepdims=True)
    l2 = jnp.min(jnp.where(pe2 == w2, lane, ROUTE_LANES), axis=-1, keepdims=True)
    tot = w1 + w2
    out = jnp.where(lane == 0, (l1 - N_GROUPS).astype(F32), 0.0)
    out = jnp.where(lane == 1, (l2 - N_GROUPS).astype(F32), out)
    out = jnp.where(lane == 2, p_sel * (w1 / tot), out)
    out = jnp.where(lane == 3, p_sel * (w2 / tot), out)
    return out


ROW_TILE_SHAPE = (D_MODEL // 128, 128)


def _store_row_tiles(ref, x):
    for s in range(ROW_TILE_SHAPE[0]):
        ref[:, s, :] = x[:, s * 128:(s + 1) * 128]


def _load_row_tiles(ref, lo, hi):
    return jnp.concatenate([ref[lo:hi, s, :] for s in range(ROW_TILE_SHAPE[0])], axis=1)


def _outproj_kernel(oa_ref, or_ref, x_ref, anw_ref, wo_ref, fnw_ref, rwh_ref, rwl_ref, rb_ref,
                    xm_ref, xn_ref, rt_ref):
    a_n = _rms_rows(oa_ref[...], anw_ref[...]).astype(BF16)
    mix = (jnp.dot(a_n, wo_ref[0:ATTN_WIDTH, :], preferred_element_type=F32)
           + jnp.dot(or_ref[...], wo_ref[ATTN_WIDTH:, :], preferred_element_type=F32))
    xm = x_ref[...] + mix
    xm_ref[...] = xm
    xn = _rms_rows(xm, fnw_ref[...])
    _store_row_tiles(xn_ref, xn)
    hi, lo = _split_bf16(xn)
    logits = (jnp.dot(hi, rwh_ref[...], preferred_element_type=F32)
              + jnp.dot(lo, rwh_ref[...], preferred_element_type=F32)
              + jnp.dot(hi, rwl_ref[...], preferred_element_type=F32)) + rb_ref[...]
    rt_ref[...] = _route(logits)


def _outproj_streams_kernel(*refs, n_first):
    first, second, rest = refs[0:3], refs[3:6], refs[6:]
    i = pl.program_id(0)

    @pl.when(i < n_first)
    def _():
        _outproj_kernel(*first, *rest)

    @pl.when(i >= n_first)
    def _():
        _outproj_kernel(*second, *rest)


def out_proj(stream_a, stream_b, attn_out_norm, w_out_bf, norm_ffn, rgw, rgb, rew, reb, tm):
    n_a, n_b = stream_a[2].shape[0], stream_b[2].shape[0]
    assert n_a % tm == 0 and n_b % tm == 0
    na, nb = n_a // tm, n_b // tm
    total = n_a + n_b
    rw = jnp.zeros((D_MODEL, ROUTE_LANES), F32).at[:, :N_GROUPS].set(rgw).at[:, N_GROUPS:N_GROUPS + N_EXPERTS].set(rew)
    rb = jnp.zeros((1, ROUTE_LANES), F32).at[0, :N_GROUPS].set(rgb).at[0, N_GROUPS:N_GROUPS + N_EXPERTS].set(reb)
    rwh, rwl = _split_bf16(rw)
    row_a = lambda w: pl.BlockSpec((tm, w), lambda i: (jnp.minimum(i, na - 1), 0))
    row_b = lambda w: pl.BlockSpec((tm, w), lambda i: (jnp.maximum(i - na, 0), 0))
    orow = lambda w: pl.BlockSpec((tm, w), lambda i: (i, 0))
    const = lambda shp: pl.BlockSpec(shp, lambda i: (0, 0))
    return pl.pallas_call(
        functools.partial(_outproj_streams_kernel, n_first=na),
        grid=(na + nb,),
        in_specs=[row_a(ATTN_WIDTH), row_a(RNN_WIDTH), row_a(D_MODEL), row_b(ATTN_WIDTH), row_b(RNN_WIDTH), row_b(D_MODEL),
                  const((1, ATTN_WIDTH)), const((D_MODEL, D_MODEL)), const((1, D_MODEL)), const((D_MODEL, ROUTE_LANES)),
                  const((D_MODEL, ROUTE_LANES)), const((1, ROUTE_LANES))],
        out_specs=[orow(D_MODEL), pl.BlockSpec((tm,) + ROW_TILE_SHAPE, lambda i: (i, 0, 0)), orow(ROUTE_LANES)],
        out_shape=[jax.ShapeDtypeStruct((total, D_MODEL), F32), jax.ShapeDtypeStruct((total,) + ROW_TILE_SHAPE, F32),
                   jax.ShapeDtypeStruct((total, ROUTE_LANES), F32)],
        compiler_params=_cparams(("arbitrary",)),
        name="out_proj",
    )(*stream_a, *stream_b, attn_out_norm.reshape(1, ATTN_WIDTH), w_out_bf, norm_ffn.reshape(1, D_MODEL), rwh, rwl, rb)


MOE_ROWS = 256


def _gather_rows(src_hbm, dst, sem, idx_ref, base, count):
    def issue(r, c):
        pltpu.make_async_copy(src_hbm.at[idx_ref[base + r]], dst.at[r], sem).start()
        return c
    lax.fori_loop(0, count, issue, 0, unroll=8)


def _wait_rows(src_hbm, dst, sem, count):
    pltpu.make_async_copy(src_hbm.at[pl.ds(0, count)], dst, sem).wait()


def _expert_kernel(be_ref, nu_ref, tok_ref, x_hbm, wg_ref, wu_ref, wd_ref, y_ref,
                   xbuf, sems, xb_ref, wg_bf, wu_bf, wd_bf):
    i = pl.program_id(0)
    n_used = nu_ref[0]
    slot = lax.rem(i, 2)

    @pl.when(i == 0)
    def _():
        _gather_rows(x_hbm, xbuf.at[0], sems.at[0], tok_ref, 0, MOE_ROWS)

    changed = jnp.logical_or(i == 0, be_ref[i] != be_ref[jnp.maximum(i - 1, 0)])

    @pl.when(jnp.logical_and(changed, i < n_used))
    def _():
        wg_bf[...] = wg_ref[0].astype(BF16)
        wu_bf[...] = wu_ref[0].astype(BF16)
        wd_bf[...] = wd_ref[0].astype(BF16)

    @pl.when(i < n_used)
    def _():
        _wait_rows(x_hbm, xbuf.at[slot], sems.at[slot], MOE_ROWS)
        xb_ref[...] = _load_row_tiles(xbuf.at[slot], 0, MOE_ROWS).astype(BF16)

    def compute():
        xb = xb_ref[...]
        hg = jnp.dot(xb, wg_bf[...], preferred_element_type=F32)
        hu = jnp.dot(xb, wu_bf[...], preferred_element_type=F32)
        hdn = (hg * jax.nn.sigmoid(hg) * hu).astype(BF16)
        _store_row_tiles(y_ref, jnp.dot(hdn, wd_bf[...], preferred_element_type=F32))

    @pl.when(i + 1 < n_used)
    def _():
        base = (i + 1) * MOE_ROWS
        for r in range(MOE_ROWS):
            pltpu.make_async_copy(x_hbm.at[tok_ref[base + r]], xbuf.at[1 - slot, r], sems.at[1 - slot]).start()
        compute()

    @pl.when(i + 1 == n_used)
    def _():
        compute()

    @pl.when(i >= n_used)
    def _():
        y_ref[...] = jnp.zeros_like(y_ref)


def expert_ffn(xn, block_expert, n_used, buf_tok, w_gate, w_up, w_down):
    nb = block_expert.shape[0]
    wmap = lambda i, be, nu, tok: (be[i], 0, 0)
    return pl.pallas_call(
        _expert_kernel,
        grid_spec=pltpu.PrefetchScalarGridSpec(
            num_scalar_prefetch=3,
            grid=(nb,),
            in_specs=[pl.BlockSpec(memory_space=pl.ANY),
                      pl.BlockSpec((1, D_MODEL, D_EXPERT), wmap),
                      pl.BlockSpec((1, D_MODEL, D_EXPERT), wmap),
                      pl.BlockSpec((1, D_EXPERT, D_MODEL), wmap)],
            out_specs=pl.BlockSpec((MOE_ROWS,) + ROW_TILE_SHAPE, lambda i, be, nu, tok: (i, 0, 0)),
            scratch_shapes=[pltpu.VMEM((2, MOE_ROWS) + ROW_TILE_SHAPE, F32), pltpu.SemaphoreType.DMA((2,)),
                            pltpu.VMEM((MOE_ROWS, D_MODEL), BF16), pltpu.VMEM((D_MODEL, D_EXPERT), BF16), pltpu.VMEM((D_MODEL, D_EXPERT), BF16),
                            pltpu.VMEM((D_EXPERT, D_MODEL), BF16)]),
        out_shape=jax.ShapeDtypeStruct((nb * MOE_ROWS,) + ROW_TILE_SHAPE, F32),
        compiler_params=_cparams(("arbitrary",)),
        name="expert_ffn",
    )(block_expert, n_used, buf_tok, xn, w_gate, w_up, w_down)


COMBINE_ROWS = 256


def _combine_kernel(pos_ref, y_hbm, xm_ref, rt_ref, o_ref, ybuf, sems, *, blk_off):
    i = pl.program_id(0)
    nsteps = pl.num_programs(0)
    slot = lax.rem(i, 2)
    rows = 2 * COMBINE_ROWS

    @pl.when(i == 0)
    def _():
        _gather_rows(y_hbm, ybuf.at[0], sems.at[0], pos_ref, blk_off * rows, rows)

    @pl.when(i + 1 < nsteps)
    def _():
        _gather_rows(y_hbm, ybuf.at[1 - slot], sems.at[1 - slot], pos_ref, (i + 1 + blk_off) * rows, rows)

    _wait_rows(y_hbm, ybuf.at[slot], sems.at[slot], rows)
    rt = rt_ref[...]
    yv = ybuf.at[slot]
    o_ref[...] = (xm_ref[...] + (rt[:, 2:3] * _load_row_tiles(yv, 0, COMBINE_ROWS)
                                 + rt[:, 3:4] * _load_row_tiles(yv, COMBINE_ROWS, rows)))


def moe_combine(pos_flat, yb, x_mid, route, row_off, n):
    assert n % COMBINE_ROWS == 0 and row_off % COMBINE_ROWS == 0
    off = row_off // COMBINE_ROWS
    row = lambda w: pl.BlockSpec((COMBINE_ROWS, w), lambda i, pos: (i + off, 0))
    return pl.pallas_call(
        functools.partial(_combine_kernel, blk_off=off),
        grid_spec=pltpu.PrefetchScalarGridSpec(
            num_scalar_prefetch=1,
            grid=(n // COMBINE_ROWS,),
            in_specs=[pl.BlockSpec(memory_space=pl.ANY), row(D_MODEL), row(ROUTE_LANES)],
            out_specs=pl.BlockSpec((COMBINE_ROWS, D_MODEL), lambda i, pos: (i, 0)),
            scratch_shapes=[pltpu.VMEM((2, 2 * COMBINE_ROWS) + ROW_TILE_SHAPE, F32), pltpu.SemaphoreType.DMA((2,))]),
        out_shape=jax.ShapeDtypeStruct((n, D_MODEL), F32),
        compiler_params=_cparams(("arbitrary",)),
        name="moe_combine",
    )(pos_flat, yb, x_mid, route)


def _moe_plan(route, n):
    a = 2 * n
    flat_e = route[:, 0:2].astype(jnp.int32).reshape(a)
    onehot = (flat_e[:, None] == jnp.arange(N_EXPERTS, dtype=jnp.int32)[None, :]).astype(jnp.int32)
    csum = jnp.cumsum(onehot, axis=0)
    counts = csum[-1]
    rank = jnp.sum(onehot * csum, axis=1) - 1
    padded = (counts + MOE_ROWS - 1) // MOE_ROWS * MOE_ROWS
    pad_end = jnp.cumsum(padded)
    pad_start = pad_end - padded
    start = jnp.cumsum(counts) - counts
    pos = (pad_start[flat_e] + rank).astype(jnp.int32)
    nb = -(-a // MOE_ROWS) + N_EXPERTS
    block_expert = jnp.minimum(jnp.sum((pad_end[None, :] <= (jnp.arange(nb) * MOE_ROWS)[:, None]).astype(jnp.int32),
                                       axis=1), N_EXPERTS - 1).astype(jnp.int32)
    n_used = (pad_end[-1] // MOE_ROWS).astype(jnp.int32).reshape(1)
    order = jnp.argsort(flat_e, stable=True)
    r = jnp.arange(nb * MOE_ROWS, dtype=jnp.int32)
    e_r = jnp.repeat(block_expert, MOE_ROWS)
    j = r - pad_start[e_r]
    buf_tok = jnp.where(j < counts[e_r], order[jnp.clip(start[e_r] + j, 0, a - 1)] // 2, 0).astype(jnp.int32)
    pos_flat = jnp.swapaxes(pos.reshape(n // COMBINE_ROWS, COMBINE_ROWS, 2), 1, 2).reshape(a)
    return block_expert, n_used, buf_tok, pos_flat


def moe_layer(x_mid, xn, route, w_gate, w_up, w_down, splits):
    block_expert, n_used, buf_tok, pos_flat = _moe_plan(route, x_mid.shape[0])
    yb = expert_ffn(xn, block_expert, n_used, buf_tok, w_gate, w_up, w_down)
    return [moe_combine(pos_flat, yb, x_mid, route, off, n) for off, n in splits]


PAGE_CHUNK = 16
LOCAL_ROWS = 8


N_SLOTS = 3


def _moba_sample_kernel(pt_ref, q_ref, kn_ref, vn_ref, ck_hbm, cv_hbm, o_ref,
                        buf, sems, s_ref, p_ref, acc_ref, *, n_pages, page_rows):
    b = pl.program_id(0)
    nb = pl.num_programs(0)
    nh = N_HEADS
    t_len = q_ref.shape[1]
    aw = q_ref.shape[2]
    rows = t_len * nh
    n_chunks = n_pages // PAGE_CHUNK
    ppb = BLOCK // page_rows
    n_blocks = n_pages // ppb
    past = n_pages * page_rows
    per_seq = 2 * n_chunks
    slot0 = lax.rem(b * per_seq, N_SLOTS)

    def slot_of(g):
        return lax.rem(slot0 + g, N_SLOTS)

    def issue(seq, g):
        seq, g = (seq + 1, g - per_seq) if g >= per_seq else (seq, g)
        cache, chunk = (ck_hbm, g) if g < n_chunks else (cv_hbm, g - n_chunks)
        slot = lax.rem(lax.rem(seq * per_seq, N_SLOTS) + g, N_SLOTS)
        for j in range(PAGE_CHUNK):
            pg = pt_ref[seq * n_pages + chunk * PAGE_CHUNK + j]
            pltpu.make_async_copy(cache.at[0, pg], buf.at[slot, j], sems.at[slot]).start()

    def issue_ahead(g):
        if g + 2 < per_seq:
            issue(b, g + 2)
        else:
            @pl.when(b + 1 < nb)
            def _():
                issue(b, g + 2)

    def wait(cache, slot):
        pltpu.make_async_copy(cache.at[0, pl.ds(0, PAGE_CHUNK)], buf.at[slot], sems.at[slot]).wait()

    @pl.when(b == 0)
    def _():
        issue(0, 0)
        issue(0, 1)

    q = q_ref[0]
    r_i = lax.broadcasted_iota(jnp.int32, (rows, aw), 0)
    c_i = lax.broadcasted_iota(jnp.int32, (rows, aw), 1)
    own = (c_i // HEAD_DIM) == lax.rem(r_i, nh)
    qrep = jnp.broadcast_to(q[:, None, :], (t_len, nh, aw)).reshape(rows, aw)
    qs = jnp.where(own, qrep * (HEAD_DIM ** -0.5 * LOG2E), 0.0).astype(BF16)

    for c in range(n_chunks):
        slot = slot_of(c)
        issue_ahead(c)
        wait(ck_hbm, slot)

        def k_page(j, cc, c=c, slot=slot):
            col0 = pl.multiple_of((c * PAGE_CHUNK + j) * page_rows, page_rows)
            kp = buf[slot, j].reshape(aw, page_rows).astype(BF16)
            s_ref[:, pl.ds(col0, page_rows)] = jnp.dot(qs, kp, preferred_element_type=F32)
            return cc
        lax.fori_loop(0, PAGE_CHUNK, k_page, 0, unroll=4)

    lane = lax.broadcasted_iota(jnp.int32, (rows, 128), 1)
    gate = jnp.full((rows, 128), -jnp.inf, F32)
    for n in range(n_blocks):
        gsum = jnp.sum(s_ref[:, n * BLOCK:(n + 1) * BLOCK], axis=-1, keepdims=True)
        gate = jnp.where(lane == n, gsum, gate)
    sel = jnp.zeros((rows, 128), F32)
    for _ in range(min(TOP_BLOCKS, n_blocks)):
        mx = jnp.max(gate, axis=-1, keepdims=True)
        first = jnp.min(jnp.where(gate == mx, lane, 128), axis=-1, keepdims=True)
        hit = lane == first
        sel = jnp.where(hit, 1.0, sel)
        gate = jnp.where(hit, -jnp.inf, gate)

    r_col = lax.broadcasted_iota(jnp.int32, (rows, 1), 0)
    head_r = lax.rem(r_col, nh)
    tok_r = (r_col // nh).astype(F32)
    slope2 = lax.bitcast_convert_type(lax.shift_left(126 - head_r, 23), F32) * LOG2E
    key_in_blk = lax.broadcasted_iota(jnp.int32, (rows, BLOCK), 1).astype(F32)
    base_bias = slope2 * key_in_blk

    def biased(n):
        chosen = sel[:, n:n + 1] > 0.0
        blk = jnp.where(chosen, slope2 * (float(n * BLOCK - past) - tok_r), NEG)
        return s_ref[:, n * BLOCK:(n + 1) * BLOCK] + (base_bias + blk)

    s_loc = _dot_nt(qs, kn_ref[0].astype(BF16))
    l_col = lax.broadcasted_iota(jnp.int32, s_loc.shape, 1)
    l_tok = lax.broadcasted_iota(jnp.int32, s_loc.shape, 0) // nh
    s_loc = jnp.where(jnp.logical_and(l_col <= l_tok, l_col < t_len),
                      s_loc + slope2 * (l_col.astype(F32) - tok_r), NEG)

    m_acc = biased(0)
    for n in range(1, n_blocks):
        m_acc = jnp.maximum(m_acc, biased(n))
    m = jnp.maximum(jnp.max(m_acc, axis=-1, keepdims=True), jnp.max(s_loc, axis=-1, keepdims=True))
    p_loc = jnp.exp2(s_loc - m)
    l_acc = jnp.zeros((rows, BLOCK), F32)
    for n in range(n_blocks):
        pb = jnp.exp2(biased(n) - m)
        l_acc = l_acc + pb
        p_ref[:, n * BLOCK:(n + 1) * BLOCK] = pb.astype(BF16)
    l = jnp.sum(l_acc, axis=-1, keepdims=True) + jnp.sum(p_loc, axis=-1, keepdims=True)
    acc_ref[...] = jnp.dot(p_loc.astype(BF16), vn_ref[0].astype(BF16), preferred_element_type=F32)

    for c in range(n_chunks):
        slot = slot_of(n_chunks + c)
        issue_ahead(n_chunks + c)
        wait(cv_hbm, slot)

        def v_page(j, cc, c=c, slot=slot):
            col0 = pl.multiple_of((c * PAGE_CHUNK + j) * page_rows, page_rows)
            vp = buf[slot, j].reshape(aw, page_rows).astype(BF16)
            acc_ref[...] += _dot_nt(p_ref[:, pl.ds(col0, page_rows)], vp)
            return cc
        lax.fori_loop(0, PAGE_CHUNK, v_page, 0, unroll=4)

    o_full = jnp.where(own, acc_ref[...] / l, 0.0)
    o_ref[0] = jnp.sum(o_full.reshape(t_len, nh, aw), axis=1)


def moba_sample(q, k, v, cache_k, cache_v, page_table):
    db, t_len, aw = q.shape
    n_pages = page_table.shape[1]
    page_rows = cache_k.shape[2]
    assert n_pages % PAGE_CHUNK == 0 and BLOCK % page_rows == 0 and page_rows % 128 == 0
    assert t_len <= LOCAL_ROWS and n_pages * page_rows // BLOCK <= 128
    rows = t_len * N_HEADS
    n_keys = n_pages * page_rows
    pages = lambda c: jnp.transpose(c, (0, 1, 3, 4, 2))
    padl = lambda a: jnp.pad(a, ((0, 0), (0, LOCAL_ROWS - t_len), (0, 0)))
    per_seq = lambda r: pl.BlockSpec((1, r, aw), lambda i, pt: (i, 0, 0))
    kern = functools.partial(_moba_sample_kernel, n_pages=n_pages, page_rows=page_rows)
    return pl.pallas_call(
        kern,
        grid_spec=pltpu.PrefetchScalarGridSpec(
            num_scalar_prefetch=1,
            grid=(db,),
            in_specs=[per_seq(t_len), per_seq(LOCAL_ROWS), per_seq(LOCAL_ROWS),
                      pl.BlockSpec(memory_space=pl.ANY), pl.BlockSpec(memory_space=pl.ANY)],
            out_specs=per_seq(t_len),
            scratch_shapes=[pltpu.VMEM((N_SLOTS, PAGE_CHUNK, N_HEADS, HEAD_DIM, page_rows), F32),
                            pltpu.SemaphoreType.DMA((N_SLOTS,)),
                            pltpu.VMEM((rows, n_keys), F32),
                            pltpu.VMEM((rows, n_keys), BF16),
                            pltpu.VMEM((rows, aw), F32)]),
        out_shape=jax.ShapeDtypeStruct((db, t_len, aw), F32),
        compiler_params=_cparams(("arbitrary",)),
        name="moba_sample",
    )(page_table.reshape(-1), q, padl(k), padl(v), pages(cache_k), pages(cache_v))


def _blockdiag(w):
    nb, c, d = w.shape
    eye = jnp.eye(nb, dtype=w.dtype)
    return (eye[:, None, :, None] * w[:, :, None, :]).reshape(nb * c, nb * d)


def _rnn_weights(conv_w, conv_b, lru_wa, lru_ba, lru_wx, lru_bx, lru_lambda, rnn_out_norm):
    r1 = lambda a: a.reshape(1, RNN_WIDTH)
    return (conv_w, r1(conv_b), _blockdiag(lru_wa).astype(BF16), r1(lru_ba),
            _blockdiag(lru_wx).astype(BF16), r1(lru_bx), r1(lru_lambda), r1(rnn_out_norm))


def rnn_seq(u, g, conv_buf, h0, weights, ts):
    b, s, w = u.shape
    assert s % ts == 0 and ts % 8 == 0
    cb8 = jnp.concatenate([jnp.zeros((b, 8 - (CONV_WIDTH - 1), w), F32), conv_buf], axis=1)
    tile = pl.BlockSpec((1, ts, w), lambda i, j: (i, j, 0))
    perb = lambda r: pl.BlockSpec((1, r, w), lambda i, j: (i, 0, 0))
    const = lambda shp: pl.BlockSpec(shp, lambda i, j: (0, 0))
    wspecs = [const((CONV_WIDTH, w)), const((1, w)), const((w, w)), const((1, w)), const((w, w)),
              const((1, w)), const((1, w)), const((1, w))]
    o, hl, cl = pl.pallas_call(
        _rnn_seq_kernel,
        grid=(b, s // ts),
        in_specs=[tile, tile, perb(8), perb(1)] + wspecs,
        out_specs=[tile, perb(1), perb(8)],
        out_shape=[jax.ShapeDtypeStruct((b, s, w), BF16), jax.ShapeDtypeStruct((b, 1, w), F32),
                   jax.ShapeDtypeStruct((b, 8, w), F32)],
        scratch_shapes=[pltpu.VMEM((ts + 8, w), F32), pltpu.VMEM((1, w), F32)],
        compiler_params=_cparams(("arbitrary", "arbitrary")),
        name="rnn_seq",
    )(u, g, cb8, h0.reshape(b, 1, w), *weights)
    return o, hl.reshape(b, w), cl[:, 8 - (CONV_WIDTH - 1):, :]


def _rnn_step_kernel(u_ref, g_ref, cb_ref, h0_ref, cw_ref, cbias_ref, wa_ref, ba_ref, wx_ref, bx_ref,
                     lam_ref, onw_ref, o_ref, hl_ref, cl_ref):
    t_len = u_ref.shape[0]
    taps = [cb_ref[j] for j in range(CONV_WIDTH - 1)] + [u_ref[t] for t in range(t_len)]
    h = h0_ref[...]
    for t in range(t_len):
        xc = cbias_ref[...]
        for j in range(CONV_WIDTH):
            xc = xc + cw_ref[j:j + 1, :] * taps[t + j]
        a, b = _lru_coeffs(xc, wa_ref, ba_ref, wx_ref, bx_ref, lam_ref)
        h = a * h + b
        o_ref[t] = _rms_rows(h * _gelu_tanh(g_ref[t]), onw_ref[...]).astype(o_ref.dtype)
    hl_ref[...] = h
    for j in range(CONV_WIDTH - 1):
        cl_ref[j] = taps[t_len + j]


def rnn_step(u, g, conv_buf, h0, weights):
    db, t_len, w = u.shape
    tm = lambda a: jnp.swapaxes(a, 0, 1)
    o, hl, cl = pl.pallas_call(
        _rnn_step_kernel,
        out_shape=[jax.ShapeDtypeStruct((t_len, db, w), BF16), jax.ShapeDtypeStruct((db, w), F32),
                   jax.ShapeDtypeStruct((CONV_WIDTH - 1, db, w), F32)],
        compiler_params=pltpu.CompilerParams(vmem_limit_bytes=VMEM_LIMIT),
        name="rnn_step",
    )(tm(u), tm(g), tm(conv_buf), h0, *weights)
    return tm(o), hl, tm(cl)


ROW_TILE = 512
SCAN_TILE = 256


def _mixers(x, attn_fn, rnn_fn, lp):
    b, t, d = x.shape
    n = b * t
    q, k, v, kb, vb, u, g, kmean = in_proj(x.reshape(n, d), lp["norm_mix"], lp["w_in"], lp["q_norm"], lp["k_norm"],
                                           min(ROW_TILE, n))
    sh = lambda a: a.reshape(b, t, -1)
    o_attn = attn_fn(sh(q), sh(k), sh(v), sh(kb), sh(vb), kmean)
    o_rnn, h_last, conv_last = rnn_fn(sh(u), sh(g))
    heads = lambda a: a.reshape(b, t, N_HEADS, HEAD_DIM)
    return o_attn.reshape(n, -1), o_rnn.reshape(n, -1), heads(k), heads(v), conv_last, h_last


def kernel(x_prompt, x_sample, cache_k, cache_v, state_conv, state_h, page_table, norm_mix, w_in, q_norm, k_norm, conv_w, conv_b, lru_wa, lru_ba, lru_wx, lru_bx, lru_lambda, attn_out_norm, rnn_out_norm, w_out, norm_ffn, router_group_w, router_group_b, router_expert_w, router_expert_b, expert_w_gate, expert_w_up, expert_w_down):
    depth = w_in.shape[0]
    yp, ys = x_prompt, x_sample
    n_p = yp.shape[0] * yp.shape[1]
    n_s = ys.shape[0] * ys.shape[1]
    outs = [[] for _ in range(8)]
    for l in range(depth):
        lp = dict(norm_mix=norm_mix[l], w_in=w_in[l].astype(BF16), q_norm=q_norm[l], k_norm=k_norm[l])
        rnn_w = _rnn_weights(conv_w[l], conv_b[l], lru_wa[l], lru_ba[l], lru_wx[l], lru_bx[l],
                             lru_lambda[l], rnn_out_norm[l])
        bp = yp.shape[0]

        def prompt_attn(q, k, v, kb, vb, kmean):
            return moba_prompt(q, kb, jnp.swapaxes(vb, 1, 2), kmean.reshape(bp, -1, ATTN_WIDTH))

        def prompt_rnn(u, g):
            return rnn_seq(u, g, jnp.zeros((bp, CONV_WIDTH - 1, RNN_WIDTH), F32),
                           jnp.zeros((bp, RNN_WIDTH), F32), rnn_w, SCAN_TILE)

        def sample_attn(q, k, v, kb, vb, kmean, l=l):
            return moba_sample(q, k, v, cache_k[l:l + 1], cache_v[l:l + 1], page_table)

        def sample_rnn(u, g, l=l):
            return rnn_step(u, g, state_conv[l], state_h[l], rnn_w)

        oa_p, or_p, kp, vp, cp, hp = _mixers(yp, prompt_attn, prompt_rnn, lp)
        oa_s, or_s, ks_, vs_, cs, hs = _mixers(ys, sample_attn, sample_rnn, lp)
        x_mid, xn, route = out_proj((oa_p, or_p, yp.reshape(n_p, D_MODEL)), (oa_s, or_s, ys.reshape(n_s, D_MODEL)),
                                    attn_out_norm[l], w_out[l].astype(BF16), norm_ffn[l], router_group_w[l],
                                    router_group_b[l], router_expert_w[l], router_expert_b[l],
                                    min(ROW_TILE, n_p, n_s))
        y_p, y_s = moe_layer(x_mid, xn, route, expert_w_gate[l], expert_w_up[l], expert_w_down[l],
                             ((0, n_p), (n_p, n_s)))
        yp, ys = y_p.reshape(yp.shape), y_s.reshape(ys.shape)
        for lst, val in zip(outs, (kp, vp, cp, hp, ks_, vs_, cs, hs)):
            lst.append(val)
    return (yp, ys) + tuple(jnp.stack(lst) for lst in outs)
```

```python
import functools

import jax
import jax.numpy as jnp
import numpy as np
from jax import lax
from jax.experimental import pallas as pl
from jax.experimental.pallas import tpu as pltpu

F32 = jnp.float32
BF16 = jnp.bfloat16

D_MODEL = 1024
ATTN_WIDTH = 512
RNN_WIDTH = 512
HEAD_DIM = 64
N_HEADS = 8
LRU_BLOCKS = 8
CONV_WIDTH = 4
LRU_C = 8.0
BLOCK = 256
TOP_BLOCKS = 3
N_GROUPS = 4
EXPERTS_PER_GROUP = 8
N_EXPERTS = 32
D_EXPERT = 512
EPS = 1e-6
D_IN = ATTN_WIDTH + 2 * ATTN_WIDTH + 2 * RNN_WIDTH
NEG = -1e30
LOG2E = 1.4426950408889634

VMEM_LIMIT = 56 * 1024 * 1024


def _cparams(sem):
    return pltpu.CompilerParams(dimension_semantics=sem, vmem_limit_bytes=VMEM_LIMIT)


def _split_bf16(a):
    hi = a.astype(BF16)
    lo = (a - hi.astype(F32)).astype(BF16)
    return hi, lo


def _inproj_kernel(x_ref, nw_ref, w_ref, qn_ref, kn_ref, gm_ref,
                   q_ref, k_ref, v_ref, kb_ref, vb_ref, u_ref, g_ref, km_ref):
    x = x_ref[...]
    ms = jnp.mean(x * x, axis=-1, keepdims=True)
    xn = x * lax.rsqrt(ms + EPS) * nw_ref[...]
    z = jnp.dot(xn.astype(BF16), w_ref[...], preferred_element_type=F32)
    aw = ATTN_WIDTH

    def head_norm(t, gain):
        hi, lo = _split_bf16(t * t)
        hm = (jnp.dot(hi, gm_ref[...], preferred_element_type=F32)
              + jnp.dot(lo, gm_ref[...], preferred_element_type=F32))
        return t * lax.rsqrt(hm + EPS) * gain

    q = head_norm(z[:, 0:aw], qn_ref[...])
    k = head_norm(z[:, aw:2 * aw], kn_ref[...])
    v = z[:, 2 * aw:3 * aw]
    q_ref[...] = q
    k_ref[...] = k
    v_ref[...] = v
    kb_ref[...] = k.astype(BF16)
    vb_ref[...] = v.astype(BF16)
    u_ref[...] = z[:, 3 * aw:3 * aw + RNN_WIDTH]
    g_ref[...] = z[:, 3 * aw + RNN_WIDTH:]
    tm = k.shape[0]
    km_ref[0] = jnp.mean(k.reshape(tm // BLOCK, BLOCK, aw), axis=1)


def _head_avg_matrix():
    r = jnp.arange(ATTN_WIDTH) // HEAD_DIM
    return jnp.where(r[:, None] == r[None, :], 1.0 / HEAD_DIM, 0.0).astype(BF16)


def in_proj(x2d, norm_w, w_in_bf, q_norm, k_norm, tm):
    n = x2d.shape[0]
    assert n % tm == 0 and tm % BLOCK == 0
    nt = n // tm
    row = lambda i: (i, 0)
    const = lambda i: (0, 0)
    wspec = lambda shp: pl.BlockSpec(shp, const)
    out_f = jax.ShapeDtypeStruct((n, ATTN_WIDTH), F32)
    out_b = jax.ShapeDtypeStruct((n, ATTN_WIDTH), BF16)
    ospec = pl.BlockSpec((tm, ATTN_WIDTH), row)
    return pl.pallas_call(
        _inproj_kernel,
        grid=(nt,),
        in_specs=[pl.BlockSpec((tm, D_MODEL), row), wspec((1, D_MODEL)), wspec((D_MODEL, D_IN)),
                  wspec((1, ATTN_WIDTH)), wspec((1, ATTN_WIDTH)), wspec((ATTN_WIDTH, ATTN_WIDTH))],
        out_specs=[ospec, ospec, ospec, ospec, ospec, ospec, ospec,
                   pl.BlockSpec((1, tm // BLOCK, ATTN_WIDTH), lambda i: (i, 0, 0))],
        out_shape=[out_f, out_f, out_f, out_b, out_b, out_f, out_f,
                   jax.ShapeDtypeStruct((nt, tm // BLOCK, ATTN_WIDTH), F32)],
        compiler_params=_cparams(("arbitrary",)),
        name="in_proj",
    )(x2d, norm_w.reshape(1, D_MODEL), w_in_bf,
      jnp.tile(q_norm, N_HEADS).reshape(1, ATTN_WIDTH), jnp.tile(k_norm, N_HEADS).reshape(1, ATTN_WIDTH),
      _head_avg_matrix())


def _gelu_tanh(x):
    return 0.5 * x * (1.0 + jnp.tanh(0.7978845608028654 * (x + 0.044715 * (x * x * x))))


def _softplus(z):
    return jnp.maximum(z, 0.0) + jnp.log1p(jnp.exp(-jnp.abs(z)))


def _lru_coeffs(xc, wa_ref, ba_ref, wx_ref, bx_ref, lam_ref):
    xb = xc.astype(BF16)
    r = jax.nn.sigmoid(jnp.dot(xb, wa_ref[...], preferred_element_type=F32) + ba_ref[...])
    i = jax.nn.sigmoid(jnp.dot(xb, wx_ref[...], preferred_element_type=F32) + bx_ref[...])
    log_a = -LRU_C * r * _softplus(-lam_ref[...])
    a = jnp.exp(log_a)
    b = jnp.sqrt(-jnp.tanh(log_a) * (a * a + 1.0)) * i * xc
    return a, b


def _rms_rows(x, gain):
    return x * lax.rsqrt(jnp.mean(x * x, axis=-1, keepdims=True) + EPS) * gain


def _rnn_seq_kernel(u_ref, g_ref, cb_ref, h0_ref, cw_ref, cbias_ref, wa_ref, ba_ref, wx_ref, bx_ref,
                    lam_ref, onw_ref, o_ref, hl_ref, cl_ref, pad_ref, h_ref):
    s = pl.program_id(1)
    ts = u_ref.shape[1]
    w = u_ref.shape[2]

    @pl.when(s == 0)
    def _():
        pad_ref[0:8, :] = cb_ref[0]
        h_ref[...] = h0_ref[0]

    u = u_ref[0]
    pad_ref[8:8 + ts, :] = u
    xc = cbias_ref[...] + cw_ref[3:4, :] * u
    for j in range(CONV_WIDTH - 1):
        xc = xc + cw_ref[j:j + 1, :] * pad_ref[5 + j:5 + j + ts, :]
    tail = pad_ref[ts:ts + 8, :]
    pad_ref[0:8, :] = tail
    cl_ref[0] = tail

    a, b = _lru_coeffs(xc, wa_ref, ba_ref, wx_ref, bx_ref, lam_ref)
    row = lax.broadcasted_iota(jnp.int32, (ts, w), 0)
    d = 1
    while d < ts:
        valid = row >= d
        a_sh = jnp.where(valid, pltpu.roll(a, d, axis=0), 1.0)
        b_sh = jnp.where(valid, pltpu.roll(b, d, axis=0), 0.0)
        b = a * b_sh + b
        a = a * a_sh
        d *= 2
    h = a * h_ref[...] + b
    hl = h[ts - 1:ts, :]
    h_ref[...] = hl
    hl_ref[0] = hl
    o = h * _gelu_tanh(g_ref[0])
    o_ref[0] = _rms_rows(o, onw_ref[...]).astype(o_ref.dtype)


def _dot_nt(a, b):
    return lax.dot_general(a, b, (((1,), (1,)), ((), ())), preferred_element_type=F32)


def _top_blocks_mask(gate, n_valid):
    nblk, tq = gate.shape
    n_iota = lax.broadcasted_iota(jnp.int32, (nblk, tq), 0)
    ok = n_iota < n_valid
    g = jnp.where(ok, gate, -jnp.inf)
    sel = jnp.zeros((nblk, tq), F32)
    for _ in range(TOP_BLOCKS):
        mx = jnp.max(g, axis=0, keepdims=True)
        first = jnp.min(jnp.where(g == mx, n_iota, nblk), axis=0, keepdims=True)
        hit = n_iota == first
        sel = jnp.where(hit, 1.0, sel)
        g = jnp.where(hit, -jnp.inf, g)
    return jnp.where(jnp.logical_and(sel > 0.0, ok), 0.0, NEG)


KEY_GROUP = 4
MAX_BLOCKS = 64
AUG = 128


def _alibi_terms():
    slopes = np.exp2(-8.0 * np.arange(1, N_HEADS + 1) / N_HEADS)
    out = np.zeros((N_HEADS, AUG - MAX_BLOCKS), np.float32)
    for h in range(N_HEADS):
        for j, val in enumerate((slopes[h] * LOG2E, slopes[h] * LOG2E * BLOCK)):
            rest = float(np.float32(val))
            for i in range(3):
                term = float(np.asarray(rest, np.float32).astype(jnp.bfloat16).astype(np.float32))
                out[h, 3 * j + i] = term
                rest -= term
    return jnp.asarray(out.reshape(N_HEADS // 2, 2, 1, AUG - MAX_BLOCKS), dtype=BF16)


def _key_aug_columns(s):
    pos = jnp.arange(s, dtype=jnp.int32)[:, None]
    col = jnp.arange(AUG, dtype=jnp.int32)[None, :]
    blk, off = pos // BLOCK, pos % BLOCK
    vals = jnp.where(col < MAX_BLOCKS, (col == blk).astype(jnp.int32),
                     jnp.where(col < MAX_BLOCKS + 3, off, jnp.where(col < MAX_BLOCKS + 6, blk, 0)))
    return vals.astype(BF16)


def _moba_prompt_kernel(q_ref, ka_ref, vt_ref, km_ref, coef_ref, o_ref, qa_ref, acc_ref, s0_ref, s1_ref):
    qi = pl.program_id(2)
    tq = q_ref.shape[1]
    nblk = km_ref.shape[1]
    t0 = pl.multiple_of(qi * tq, BLOCK)
    q = q_ref[0]
    km_hi, km_lo = _split_bf16(km_ref[0])
    lane = lax.broadcasted_iota(jnp.int32, q.shape, 1)
    key_i = lax.broadcasted_iota(jnp.int32, (BLOCK, tq), 0)
    qry_i = lax.broadcasted_iota(jnp.int32, (BLOCK, tq), 1)
    causal = key_i <= qry_i
    kd = ka_ref[0, 0, pl.ds(t0, BLOCK), :]
    nomask = jnp.zeros((tq, MAX_BLOCKS), BF16)

    init = []
    for e in range(2):
        coef = jnp.broadcast_to(coef_ref[0, e], (tq, AUG - MAX_BLOCKS))
        qh = jnp.where((lane // HEAD_DIM) == e, q, 0.0)
        q_hi, q_lo = _split_bf16(qh)
        gate = _dot_nt(km_hi, q_hi) + _dot_nt(km_hi, q_lo) + _dot_nt(km_lo, q_hi)
        mask_t = _top_blocks_mask(gate, qi)
        mask = jnp.concatenate([mask_t, jnp.zeros((128 - nblk, tq), F32)], axis=0).T[:, :MAX_BLOCKS]
        qs = (qh * (HEAD_DIM ** -0.5 * LOG2E)).astype(BF16)
        qa_ref[e * tq:(e + 1) * tq, :] = jnp.concatenate([qs, mask.astype(BF16), coef], axis=1)
        qd = jnp.concatenate([qs, nomask, coef], axis=1)
        s = jnp.where(causal, _dot_nt(kd, qd), NEG)
        m = jnp.max(s, axis=0, keepdims=True)
        p = jnp.exp2(s - m)
        vt = vt_ref[0, HEAD_DIM * e:HEAD_DIM * (e + 1), pl.ds(t0, BLOCK)]
        acc_ref[e] = jnp.dot(vt, p.astype(BF16), preferred_element_type=F32)
        init += [m, jnp.sum(p, axis=0, keepdims=True)]

    gk = KEY_GROUP * BLOCK

    def scores(g, s_ref):
        base = pl.multiple_of(g * gk, gk)
        s = _dot_nt(ka_ref[0, 0, pl.ds(base, gk), :], qa_ref[...])
        s_ref[...] = s
        return jnp.max(s, axis=0, keepdims=True)

    def softmax_pv(g, s_ref, m_old, l_old, cmax):
        base = pl.multiple_of(g * gk, gk)
        m_new = jnp.maximum(m_old, cmax)
        alpha = jnp.exp2(m_old - m_new)
        p = jnp.exp2(s_ref[...] - m_new)
        pb = p.astype(BF16)
        for e in range(2):
            vt = vt_ref[0, HEAD_DIM * e:HEAD_DIM * (e + 1), pl.ds(base, gk)]
            acc_ref[e] = (alpha[:, e * tq:(e + 1) * tq] * acc_ref[e]
                          + jnp.dot(vt, pb[:, e * tq:(e + 1) * tq], preferred_element_type=F32))
        return m_new, alpha * l_old + jnp.sum(p, axis=0, keepdims=True)

    n_groups = lax.shift_right_logical(qi + (KEY_GROUP - 1), KEY_GROUP.bit_length() - 1)
    n_pairs = lax.shift_right_logical(n_groups + 1, 1)

    def pair(gg, carry, prefetch):
        m_old, l_old, cmax0 = carry
        g0 = 2 * gg
        cmax1 = scores(g0 + 1, s1_ref)
        m_mid, l_mid = softmax_pv(g0, s0_ref, m_old, l_old, cmax0)
        cmax2 = scores(g0 + 2, s0_ref) if prefetch else cmax0
        m_new, l_new = softmax_pv(g0 + 1, s1_ref, m_mid, l_mid, cmax1)
        return m_new, l_new, cmax2

    m0 = jnp.concatenate(init[0::2], axis=1)
    l0 = jnp.concatenate(init[1::2], axis=1)

    @pl.when(qi > 0)
    def _():
        cmax0 = scores(0, s0_ref)
        carry = lax.fori_loop(0, n_pairs - 1, lambda gg, c: pair(gg, c, True), (m0, l0, cmax0))
        _, l_fin, _ = pair(n_pairs - 1, carry, False)
        o_t = jnp.concatenate([acc_ref[e] / l_fin[:, e * tq:(e + 1) * tq] for e in range(2)], axis=0)
        o_ref[0] = o_t.T

    @pl.when(qi == 0)
    def _():
        o_t = jnp.concatenate([acc_ref[e] / l0[:, e * tq:(e + 1) * tq] for e in range(2)], axis=0)
        o_ref[0] = o_t.T


def moba_prompt(q, kb, vt, kmean):
    b, s, aw = q.shape
    nblk = s // BLOCK
    hp = aw // 128
    assert nblk <= MAX_BLOCKS and nblk % (2 * KEY_GROUP) == 0 and KEY_GROUP & (KEY_GROUP - 1) == 0
    ka = jnp.concatenate([jnp.swapaxes(kb.reshape(b, s, hp, 128), 1, 2),
                          jnp.broadcast_to(_key_aug_columns(s), (b, hp, s, AUG))], axis=-1)
    return pl.pallas_call(
        _moba_prompt_kernel,
        grid=(b, hp, nblk),
        in_specs=[pl.BlockSpec((1, BLOCK, 128), lambda i, h, j: (i, j, h)),
                  pl.BlockSpec((1, 1, s, 128 + AUG), lambda i, h, j: (i, h, 0, 0)),
                  pl.BlockSpec((1, 128, s), lambda i, h, j: (i, h, 0)),
                  pl.BlockSpec((1, nblk, 128), lambda i, h, j: (i, 0, h)),
                  pl.BlockSpec((1, 2, 1, AUG - MAX_BLOCKS), lambda i, h, j: (h, 0, 0, 0))],
        out_specs=pl.BlockSpec((1, BLOCK, 128), lambda i, h, j: (i, j, h)),
        out_shape=jax.ShapeDtypeStruct((b, s, aw), F32),
        scratch_shapes=[pltpu.VMEM((2 * BLOCK, 128 + AUG), BF16), pltpu.VMEM((2, HEAD_DIM, BLOCK), F32),
                        pltpu.VMEM((KEY_GROUP * BLOCK, 2 * BLOCK), F32),
                        pltpu.VMEM((KEY_GROUP * BLOCK, 2 * BLOCK), F32)],
        compiler_params=_cparams(("arbitrary", "arbitrary", "arbitrary")),
        name="moba_prompt",
    )(q, ka, vt, kmean, _alibi_terms())


ROUTE_LANES = 128


def _route(logits):
    lane = lax.broadcasted_iota(jnp.int32, logits.shape, 1)
    ninf = -jnp.inf
    gl = jnp.where(lane < N_GROUPS, logits, ninf)
    gmax = jnp.max(gl, axis=-1, keepdims=True)
    gsel = jnp.min(jnp.where(gl == gmax, lane, ROUTE_LANES), axis=-1, keepdims=True)
    p_sel = 1.0 / jnp.sum(jnp.exp(gl - gmax), axis=-1, keepdims=True)
    lo = N_GROUPS + EXPERTS_PER_GROUP * gsel
    emask = jnp.logical_and(lane >= lo, lane < lo + EXPERTS_PER_GROUP)
    el = jnp.where(emask, logits, ninf)
    emax = jnp.max(el, axis=-1, keepdims=True)
    pe_un = jnp.exp(el - emax)
    pe = jnp.where(emask, pe_un / jnp.sum(pe_un, axis=-1, keepdims=True), -1.0)
    w1 = jnp.max(pe, axis=-1, keepdims=True)
    l1 = jnp.min(jnp.where(pe == w1, lane, ROUTE_LANES), axis=-1, keepdims=True)
    pe2 = jnp.where(lane == l1, -1.0, pe)
    w2 = jnp.max(pe2, axis=-1, keepdims=True)
    l2 = jnp.min(jnp.where(pe2 == w2, lane, ROUTE_LANES), axis=-1, keepdims=True)
    tot = w1 + w2
    out = jnp.where(lane == 0, (l1 - N_GROUPS).astype(F32), 0.0)
    out = jnp.where(lane == 1, (l2 - N_GROUPS).astype(F32), out)
    out = jnp.where(lane == 2, p_sel * (w1 / tot), out)
    out = jnp.where(lane == 3, p_sel * (w2 / tot), out)
    return out


def _outproj_kernel(oa_ref, or_ref, x_ref, anw_ref, wo_ref, fnw_ref, rwh_ref, rwl_ref, rb_ref,
                    xm_ref, xn_ref, rt_ref):
    a_n = _rms_rows(oa_ref[...], anw_ref[...]).astype(BF16)
    mix = (jnp.dot(a_n, wo_ref[0:ATTN_WIDTH, :], preferred_element_type=F32)
           + jnp.dot(or_ref[...], wo_ref[ATTN_WIDTH:, :], preferred_element_type=F32))
    xm = x_ref[...] + mix
    xm_ref[...] = xm
    xn = _rms_rows(xm, fnw_ref[...])
    xn_ref[...] = xn
    hi, lo = _split_bf16(xn)
    logits = (jnp.dot(hi, rwh_ref[...], preferred_element_type=F32)
              + jnp.dot(lo, rwh_ref[...], preferred_element_type=F32)
              + jnp.dot(hi, rwl_ref[...], preferred_element_type=F32)) + rb_ref[...]
    rt_ref[...] = _route(logits)


def _outproj_streams_kernel(*refs, n_first):
    first, second, rest = refs[0:3], refs[3:6], refs[6:]
    i = pl.program_id(0)

    @pl.when(i < n_first)
    def _():
        _outproj_kernel(*first, *rest)

    @pl.when(i >= n_first)
    def _():
        _outproj_kernel(*second, *rest)


def out_proj(stream_a, stream_b, attn_out_norm, w_out_bf, norm_ffn, rgw, rgb, rew, reb, tm):
    n_a, n_b = stream_a[2].shape[0], stream_b[2].shape[0]
    assert n_a % tm == 0 and n_b % tm == 0
    na, nb = n_a // tm, n_b // tm
    total = n_a + n_b
    rw = jnp.zeros((D_MODEL, ROUTE_LANES), F32).at[:, :N_GROUPS].set(rgw).at[:, N_GROUPS:N_GROUPS + N_EXPERTS].set(rew)
    rb = jnp.zeros((1, ROUTE_LANES), F32).at[0, :N_GROUPS].set(rgb).at[0, N_GROUPS:N_GROUPS + N_EXPERTS].set(reb)
    rwh, rwl = _split_bf16(rw)
    row_a = lambda w: pl.BlockSpec((tm, w), lambda i: (jnp.minimum(i, na - 1), 0))
    row_b = lambda w: pl.BlockSpec((tm, w), lambda i: (jnp.maximum(i - na, 0), 0))
    orow = lambda w: pl.BlockSpec((tm, w), lambda i: (i, 0))
    const = lambda shp: pl.BlockSpec(shp, lambda i: (0, 0))
    return pl.pallas_call(
        functools.partial(_outproj_streams_kernel, n_first=na),
        grid=(na + nb,),
        in_specs=[row_a(ATTN_WIDTH), row_a(RNN_WIDTH), row_a(D_MODEL), row_b(ATTN_WIDTH), row_b(RNN_WIDTH), row_b(D_MODEL),
                  const((1, ATTN_WIDTH)), const((D_MODEL, D_MODEL)), const((1, D_MODEL)), const((D_MODEL, ROUTE_LANES)),
                  const((D_MODEL, ROUTE_LANES)), const((1, ROUTE_LANES))],
        out_specs=[orow(D_MODEL), orow(D_MODEL), orow(ROUTE_LANES)],
        out_shape=[jax.ShapeDtypeStruct((total, D_MODEL), F32), jax.ShapeDtypeStruct((total, D_MODEL), F32),
                   jax.ShapeDtypeStruct((total, ROUTE_LANES), F32)],
        compiler_params=_cparams(("arbitrary",)),
        name="out_proj",
    )(*stream_a, *stream_b, attn_out_norm.reshape(1, ATTN_WIDTH), w_out_bf, norm_ffn.reshape(1, D_MODEL), rwh, rwl, rb)


MOE_ROWS = 256


def _gather_rows(src_hbm, dst, sem, idx_ref, base, count):
    def issue(r, c):
        pltpu.make_async_copy(src_hbm.at[pl.ds(idx_ref[base + r], 1), :], dst.at[pl.ds(r, 1), :], sem).start()
        return c
    lax.fori_loop(0, count, issue, 0, unroll=8)


def _wait_rows(src_hbm, dst, sem, count):
    pltpu.make_async_copy(src_hbm.at[pl.ds(0, count), :], dst, sem).wait()


def _expert_kernel(be_ref, nu_ref, tok_ref, x_hbm, wg_ref, wu_ref, wd_ref, y_ref,
                   xbuf, sems, xb_ref, wg_bf, wu_bf, wd_bf):
    i = pl.program_id(0)
    n_used = nu_ref[0]
    slot = lax.rem(i, 2)

    @pl.when(i == 0)
    def _():
        _gather_rows(x_hbm, xbuf.at[0], sems.at[0], tok_ref, 0, MOE_ROWS)

    changed = jnp.logical_or(i == 0, be_ref[i] != be_ref[jnp.maximum(i - 1, 0)])

    @pl.when(jnp.logical_and(changed, i < n_used))
    def _():
        wg_bf[...] = wg_ref[0].astype(BF16)
        wu_bf[...] = wu_ref[0].astype(BF16)
        wd_bf[...] = wd_ref[0].astype(BF16)

    @pl.when(i < n_used)
    def _():
        _wait_rows(x_hbm, xbuf.at[slot], sems.at[slot], MOE_ROWS)
        xb_ref[...] = xbuf[slot].astype(BF16)

    def compute():
        xb = xb_ref[...]
        hg = jnp.dot(xb, wg_bf[...], preferred_element_type=F32)
        hu = jnp.dot(xb, wu_bf[...], preferred_element_type=F32)
        hdn = (hg * jax.nn.sigmoid(hg) * hu).astype(BF16)
        y_ref[...] = jnp.dot(hdn, wd_bf[...], preferred_element_type=F32)

    @pl.when(i + 1 < n_used)
    def _():
        base = (i + 1) * MOE_ROWS
        for r in range(MOE_ROWS):
            pltpu.make_async_copy(x_hbm.at[pl.ds(tok_ref[base + r], 1), :],
                                  xbuf.at[1 - slot, pl.ds(r, 1), :], sems.at[1 - slot]).start()
        compute()

    @pl.when(i + 1 == n_used)
    def _():
        compute()

    @pl.when(i >= n_used)
    def _():
        y_ref[...] = jnp.zeros_like(y_ref)


def expert_ffn(xn, block_expert, n_used, buf_tok, w_gate, w_up, w_down):
    nb = block_expert.shape[0]
    wmap = lambda i, be, nu, tok: (be[i], 0, 0)
    return pl.pallas_call(
        _expert_kernel,
        grid_spec=pltpu.PrefetchScalarGridSpec(
            num_scalar_prefetch=3,
            grid=(nb,),
            in_specs=[pl.BlockSpec(memory_space=pl.ANY),
                      pl.BlockSpec((1, D_MODEL, D_EXPERT), wmap),
                      pl.BlockSpec((1, D_MODEL, D_EXPERT), wmap),
                      pl.BlockSpec((1, D_EXPERT, D_MODEL), wmap)],
            out_specs=pl.BlockSpec((MOE_ROWS, D_MODEL), lambda i, be, nu, tok: (i, 0)),
            scratch_shapes=[pltpu.VMEM((2, MOE_ROWS, D_MODEL), F32), pltpu.SemaphoreType.DMA((2,)),
                            pltpu.VMEM((MOE_ROWS, D_MODEL), BF16), pltpu.VMEM((D_MODEL, D_EXPERT), BF16), pltpu.VMEM((D_MODEL, D_EXPERT), BF16),
                            pltpu.VMEM((D_EXPERT, D_MODEL), BF16)]),
        out_shape=jax.ShapeDtypeStruct((nb * MOE_ROWS, D_MODEL), F32),
        compiler_params=_cparams(("arbitrary",)),
        name="expert_ffn",
    )(block_expert, n_used, buf_tok, xn, w_gate, w_up, w_down)


COMBINE_ROWS = 256


def _combine_kernel(pos_ref, y_hbm, xm_ref, rt_ref, o_ref, ybuf, sems, *, blk_off):
    i = pl.program_id(0)
    nsteps = pl.num_programs(0)
    slot = lax.rem(i, 2)
    rows = 2 * COMBINE_ROWS

    @pl.when(i == 0)
    def _():
        _gather_rows(y_hbm, ybuf.at[0], sems.at[0], pos_ref, blk_off * rows, rows)

    @pl.when(i + 1 < nsteps)
    def _():
        _gather_rows(y_hbm, ybuf.at[1 - slot], sems.at[1 - slot], pos_ref, (i + 1 + blk_off) * rows, rows)

    _wait_rows(y_hbm, ybuf.at[slot], sems.at[slot], rows)
    rt = rt_ref[...]
    o_ref[...] = (xm_ref[...] + (rt[:, 2:3] * ybuf[slot, 0:COMBINE_ROWS, :]
                                 + rt[:, 3:4] * ybuf[slot, COMBINE_ROWS:rows, :]))


def moe_combine(pos_flat, yb, x_mid, route, row_off, n):
    assert n % COMBINE_ROWS == 0 and row_off % COMBINE_ROWS == 0
    off = row_off // COMBINE_ROWS
    row = lambda w: pl.BlockSpec((COMBINE_ROWS, w), lambda i, pos: (i + off, 0))
    return pl.pallas_call(
        functools.partial(_combine_kernel, blk_off=off),
        grid_spec=pltpu.PrefetchScalarGridSpec(
            num_scalar_prefetch=1,
            grid=(n // COMBINE_ROWS,),
            in_specs=[pl.BlockSpec(memory_space=pl.ANY), row(D_MODEL), row(ROUTE_LANES)],
            out_specs=pl.BlockSpec((COMBINE_ROWS, D_MODEL), lambda i, pos: (i, 0)),
            scratch_shapes=[pltpu.VMEM((2, 2 * COMBINE_ROWS, D_MODEL), F32), pltpu.SemaphoreType.DMA((2,))]),
        out_shape=jax.ShapeDtypeStruct((n, D_MODEL), F32),
        compiler_params=_cparams(("arbitrary",)),
        name="moe_combine",
    )(pos_flat, yb, x_mid, route)


def _moe_plan(route, n):
    a = 2 * n
    flat_e = route[:, 0:2].astype(jnp.int32).reshape(a)
    onehot = (flat_e[:, None] == jnp.arange(N_EXPERTS, dtype=jnp.int32)[None, :]).astype(jnp.int32)
    csum = jnp.cumsum(onehot, axis=0)
    counts = csum[-1]
    rank = jnp.sum(onehot * csum, axis=1) - 1
    padded = (counts + MOE_ROWS - 1) // MOE_ROWS * MOE_ROWS
    pad_end = jnp.cumsum(padded)
    pad_start = pad_end - padded
    start = jnp.cumsum(counts) - counts
    pos = (pad_start[flat_e] + rank).astype(jnp.int32)
    nb = -(-a // MOE_ROWS) + N_EXPERTS
    block_expert = jnp.minimum(jnp.sum((pad_end[None, :] <= (jnp.arange(nb) * MOE_ROWS)[:, None]).astype(jnp.int32),
                                       axis=1), N_EXPERTS - 1).astype(jnp.int32)
    n_used = (pad_end[-1] // MOE_ROWS).astype(jnp.int32).reshape(1)
    order = jnp.argsort(flat_e, stable=True)
    r = jnp.arange(nb * MOE_ROWS, dtype=jnp.int32)
    e_r = jnp.repeat(block_expert, MOE_ROWS)
    j = r - pad_start[e_r]
    buf_tok = jnp.where(j < counts[e_r], order[jnp.clip(start[e_r] + j, 0, a - 1)] // 2, 0).astype(jnp.int32)
    pos_flat = jnp.swapaxes(pos.reshape(n // COMBINE_ROWS, COMBINE_ROWS, 2), 1, 2).reshape(a)
    return block_expert, n_used, buf_tok, pos_flat


def moe_layer(x_mid, xn, route, w_gate, w_up, w_down, splits):
    block_expert, n_used, buf_tok, pos_flat = _moe_plan(route, x_mid.shape[0])
    yb = expert_ffn(xn, block_expert, n_used, buf_tok, w_gate, w_up, w_down)
    return [moe_combine(pos_flat, yb, x_mid, route, off, n) for off, n in splits]


PAGE_CHUNK = 16
LOCAL_ROWS = 8


N_SLOTS = 3


def _moba_sample_kernel(pt_ref, q_ref, kn_ref, vn_ref, ck_hbm, cv_hbm, o_ref,
                        buf, sems, s_ref, p_ref, acc_ref, *, n_pages, page_rows):
    b = pl.program_id(0)
    nb = pl.num_programs(0)
    nh = N_HEADS
    t_len = q_ref.shape[1]
    aw = q_ref.shape[2]
    rows = t_len * nh
    n_chunks = n_pages // PAGE_CHUNK
    ppb = BLOCK // page_rows
    n_blocks = n_pages // ppb
    past = n_pages * page_rows
    per_seq = 2 * n_chunks
    slot0 = lax.rem(b * per_seq, N_SLOTS)

    def slot_of(g):
        return lax.rem(slot0 + g, N_SLOTS)

    def issue(seq, g):
        seq, g = (seq + 1, g - per_seq) if g >= per_seq else (seq, g)
        cache, chunk = (ck_hbm, g) if g < n_chunks else (cv_hbm, g - n_chunks)
        slot = lax.rem(lax.rem(seq * per_seq, N_SLOTS) + g, N_SLOTS)
        for j in range(PAGE_CHUNK):
            pg = pt_ref[seq * n_pages + chunk * PAGE_CHUNK + j]
            pltpu.make_async_copy(cache.at[0, pg], buf.at[slot, j], sems.at[slot]).start()

    def issue_ahead(g):
        if g + 2 < per_seq:
            issue(b, g + 2)
        else:
            @pl.when(b + 1 < nb)
            def _():
                issue(b, g + 2)

    def wait(cache, slot):
        pltpu.make_async_copy(cache.at[0, pl.ds(0, PAGE_CHUNK)], buf.at[slot], sems.at[slot]).wait()

    @pl.when(b == 0)
    def _():
        issue(0, 0)
        issue(0, 1)

    q = q_ref[0]
    r_i = lax.broadcasted_iota(jnp.int32, (rows, aw), 0)
    c_i = lax.broadcasted_iota(jnp.int32, (rows, aw), 1)
    own = (c_i // HEAD_DIM) == lax.rem(r_i, nh)
    qrep = jnp.broadcast_to(q[:, None, :], (t_len, nh, aw)).reshape(rows, aw)
    qs = jnp.where(own, qrep * (HEAD_DIM ** -0.5 * LOG2E), 0.0).astype(BF16)

    for c in range(n_chunks):
        slot = slot_of(c)
        issue_ahead(c)
        wait(ck_hbm, slot)

        def k_page(j, cc, c=c, slot=slot):
            col0 = pl.multiple_of((c * PAGE_CHUNK + j) * page_rows, page_rows)
            kp = buf[slot, j].reshape(aw, page_rows).astype(BF16)
            s_ref[:, pl.ds(col0, page_rows)] = jnp.dot(qs, kp, preferred_element_type=F32)
            return cc
        lax.fori_loop(0, PAGE_CHUNK, k_page, 0, unroll=4)

    lane = lax.broadcasted_iota(jnp.int32, (rows, 128), 1)
    gate = jnp.full((rows, 128), -jnp.inf, F32)
    for n in range(n_blocks):
        gsum = jnp.sum(s_ref[:, n * BLOCK:(n + 1) * BLOCK], axis=-1, keepdims=True)
        gate = jnp.where(lane == n, gsum, gate)
    sel = jnp.zeros((rows, 128), F32)
    for _ in range(min(TOP_BLOCKS, n_blocks)):
        mx = jnp.max(gate, axis=-1, keepdims=True)
        first = jnp.min(jnp.where(gate == mx, lane, 128), axis=-1, keepdims=True)
        hit = lane == first
        sel = jnp.where(hit, 1.0, sel)
        gate = jnp.where(hit, -jnp.inf, gate)

    r_col = lax.broadcasted_iota(jnp.int32, (rows, 1), 0)
    head_r = lax.rem(r_col, nh)
    tok_r = (r_col // nh).astype(F32)
    slope2 = lax.bitcast_convert_type(lax.shift_left(126 - head_r, 23), F32) * LOG2E
    key_in_blk = lax.broadcasted_iota(jnp.int32, (rows, BLOCK), 1).astype(F32)
    base_bias = slope2 * key_in_blk

    def biased(n):
        chosen = sel[:, n:n + 1] > 0.0
        blk = jnp.where(chosen, slope2 * (float(n * BLOCK - past) - tok_r), NEG)
        return s_ref[:, n * BLOCK:(n + 1) * BLOCK] + (base_bias + blk)

    s_loc = _dot_nt(qs, kn_ref[0].astype(BF16))
    l_col = lax.broadcasted_iota(jnp.int32, s_loc.shape, 1)
    l_tok = lax.broadcasted_iota(jnp.int32, s_loc.shape, 0) // nh
    s_loc = jnp.where(jnp.logical_and(l_col <= l_tok, l_col < t_len),
                      s_loc + slope2 * (l_col.astype(F32) - tok_r), NEG)

    m_acc = biased(0)
    for n in range(1, n_blocks):
        m_acc = jnp.maximum(m_acc, biased(n))
    m = jnp.maximum(jnp.max(m_acc, axis=-1, keepdims=True), jnp.max(s_loc, axis=-1, keepdims=True))
    p_loc = jnp.exp2(s_loc - m)
    l_acc = jnp.zeros((rows, BLOCK), F32)
    for n in range(n_blocks):
        pb = jnp.exp2(biased(n) - m)
        l_acc = l_acc + pb
        p_ref[:, n * BLOCK:(n + 1) * BLOCK] = pb.astype(BF16)
    l = jnp.sum(l_acc, axis=-1, keepdims=True) + jnp.sum(p_loc, axis=-1, keepdims=True)
    acc_ref[...] = jnp.dot(p_loc.astype(BF16), vn_ref[0].astype(BF16), preferred_element_type=F32)

    for c in range(n_chunks):
        slot = slot_of(n_chunks + c)
        issue_ahead(n_chunks + c)
        wait(cv_hbm, slot)

        def v_page(j, cc, c=c, slot=slot):
            col0 = pl.multiple_of((c * PAGE_CHUNK + j) * page_rows, page_rows)
            vp = buf[slot, j].reshape(aw, page_rows).astype(BF16)
            acc_ref[...] += _dot_nt(p_ref[:, pl.ds(col0, page_rows)], vp)
            return cc
        lax.fori_loop(0, PAGE_CHUNK, v_page, 0, unroll=4)

    o_full = jnp.where(own, acc_ref[...] / l, 0.0)
    o_ref[0] = jnp.sum(o_full.reshape(t_len, nh, aw), axis=1)


def moba_sample(q, k, v, cache_k, cache_v, page_table):
    db, t_len, aw = q.shape
    n_pages = page_table.shape[1]
    page_rows = cache_k.shape[2]
    assert n_pages % PAGE_CHUNK == 0 and BLOCK % page_rows == 0 and page_rows % 128 == 0
    assert t_len <= LOCAL_ROWS and n_pages * page_rows // BLOCK <= 128
    rows = t_len * N_HEADS
    n_keys = n_pages * page_rows
    pages = lambda c: jnp.transpose(c, (0, 1, 3, 4, 2))
    padl = lambda a: jnp.pad(a, ((0, 0), (0, LOCAL_ROWS - t_len), (0, 0)))
    per_seq = lambda r: pl.BlockSpec((1, r, aw), lambda i, pt: (i, 0, 0))
    kern = functools.partial(_moba_sample_kernel, n_pages=n_pages, page_rows=page_rows)
    return pl.pallas_call(
        kern,
        grid_spec=pltpu.PrefetchScalarGridSpec(
            num_scalar_prefetch=1,
            grid=(db,),
            in_specs=[per_seq(t_len), per_seq(LOCAL_ROWS), per_seq(LOCAL_ROWS),
                      pl.BlockSpec(memory_space=pl.ANY), pl.BlockSpec(memory_space=pl.ANY)],
            out_specs=per_seq(t_len),
            scratch_shapes=[pltpu.VMEM((N_SLOTS, PAGE_CHUNK, N_HEADS, HEAD_DIM, page_rows), F32),
                            pltpu.SemaphoreType.DMA((N_SLOTS,)),
                            pltpu.VMEM((rows, n_keys), F32),
                            pltpu.VMEM((rows, n_keys), BF16),
                            pltpu.VMEM((rows, aw), F32)]),
        out_shape=jax.ShapeDtypeStruct((db, t_len, aw), F32),
        compiler_params=_cparams(("arbitrary",)),
        name="moba_sample",
    )(page_table.reshape(-1), q, padl(k), padl(v), pages(cache_k), pages(cache_v))


def _blockdiag(w):
    nb, c, d = w.shape
    eye = jnp.eye(nb, dtype=w.dtype)
    return (eye[:, None, :, None] * w[:, :, None, :]).reshape(nb * c, nb * d)


def _rnn_weights(conv_w, conv_b, lru_wa, lru_ba, lru_wx, lru_bx, lru_lambda, rnn_out_norm):
    r1 = lambda a: a.reshape(1, RNN_WIDTH)
    return (conv_w, r1(conv_b), _blockdiag(lru_wa).astype(BF16), r1(lru_ba),
            _blockdiag(lru_wx).astype(BF16), r1(lru_bx), r1(lru_lambda), r1(rnn_out_norm))


def rnn_seq(u, g, conv_buf, h0, weights, ts):
    b, s, w = u.shape
    assert s % ts == 0 and ts % 8 == 0
    cb8 = jnp.concatenate([jnp.zeros((b, 8 - (CONV_WIDTH - 1), w), F32), conv_buf], axis=1)
    tile = pl.BlockSpec((1, ts, w), lambda i, j: (i, j, 0))
    perb = lambda r: pl.BlockSpec((1, r, w), lambda i, j: (i, 0, 0))
    const = lambda shp: pl.BlockSpec(shp, lambda i, j: (0, 0))
    wspecs = [const((CONV_WIDTH, w)), const((1, w)), const((w, w)), const((1, w)), const((w, w)),
              const((1, w)), const((1, w)), const((1, w))]
    o, hl, cl = pl.pallas_call(
        _rnn_seq_kernel,
        grid=(b, s // ts),
        in_specs=[tile, tile, perb(8), perb(1)] + wspecs,
        out_specs=[tile, perb(1), perb(8)],
        out_shape=[jax.ShapeDtypeStruct((b, s, w), BF16), jax.ShapeDtypeStruct((b, 1, w), F32),
                   jax.ShapeDtypeStruct((b, 8, w), F32)],
        scratch_shapes=[pltpu.VMEM((ts + 8, w), F32), pltpu.VMEM((1, w), F32)],
        compiler_params=_cparams(("arbitrary", "arbitrary")),
        name="rnn_seq",
    )(u, g, cb8, h0.reshape(b, 1, w), *weights)
    return o, hl.reshape(b, w), cl[:, 8 - (CONV_WIDTH - 1):, :]


def _rnn_step_kernel(u_ref, g_ref, cb_ref, h0_ref, cw_ref, cbias_ref, wa_ref, ba_ref, wx_ref, bx_ref,
                     lam_ref, onw_ref, o_ref, hl_ref, cl_ref):
    t_len = u_ref.shape[0]
    taps = [cb_ref[j] for j in range(CONV_WIDTH - 1)] + [u_ref[t] for t in range(t_len)]
    h = h0_ref[...]
    for t in range(t_len):
        xc = cbias_ref[...]
        for j in range(CONV_WIDTH):
            xc = xc + cw_ref[j:j + 1, :] * taps[t + j]
        a, b = _lru_coeffs(xc, wa_ref, ba_ref, wx_ref, bx_ref, lam_ref)
        h = a * h + b
        o_ref[t] = _rms_rows(h * _gelu_tanh(g_ref[t]), onw_ref[...]).astype(o_ref.dtype)
    hl_ref[...] = h
    for j in range(CONV_WIDTH - 1):
        cl_ref[j] = taps[t_len + j]


def rnn_step(u, g, conv_buf, h0, weights):
    db, t_len, w = u.shape
    tm = lambda a: jnp.swapaxes(a, 0, 1)
    o, hl, cl = pl.pallas_call(
        _rnn_step_kernel,
        out_shape=[jax.ShapeDtypeStruct((t_len, db, w), BF16), jax.ShapeDtypeStruct((db, w), F32),
                   jax.ShapeDtypeStruct((CONV_WIDTH - 1, db, w), F32)],
        compiler_params=pltpu.CompilerParams(vmem_limit_bytes=VMEM_LIMIT),
        name="rnn_step",
    )(tm(u), tm(g), tm(conv_buf), h0, *weights)
    return tm(o), hl, tm(cl)


ROW_TILE = 512
SCAN_TILE = 256


def _mixers(x, attn_fn, rnn_fn, lp):
    b, t, d = x.shape
    n = b * t
    q, k, v, kb, vb, u, g, kmean = in_proj(x.reshape(n, d), lp["norm_mix"], lp["w_in"], lp["q_norm"], lp["k_norm"],
                                           min(ROW_TILE, n))
    sh = lambda a: a.reshape(b, t, -1)
    o_attn = attn_fn(sh(q), sh(k), sh(v), sh(kb), sh(vb), kmean)
    o_rnn, h_last, conv_last = rnn_fn(sh(u), sh(g))
    heads = lambda a: a.reshape(b, t, N_HEADS, HEAD_DIM)
    return o_attn.reshape(n, -1), o_rnn.reshape(n, -1), heads(k), heads(v), conv_last, h_last


def kernel(x_prompt, x_sample, cache_k, cache_v, state_conv, state_h, page_table, norm_mix, w_in, q_norm, k_norm, conv_w, conv_b, lru_wa, lru_ba, lru_wx, lru_bx, lru_lambda, attn_out_norm, rnn_out_norm, w_out, norm_ffn, router_group_w, router_group_b, router_expert_w, router_expert_b, expert_w_gate, expert_w_up, expert_w_down):
    depth = w_in.shape[0]
    yp, ys = x_prompt, x_sample
    n_p = yp.shape[0] * yp.shape[1]
    n_s = ys.shape[0] * ys.shape[1]
    outs = [[] for _ in range(8)]
    for l in range(depth):
        lp = dict(norm_mix=norm_mix[l], w_in=w_in[l].astype(BF16), q_norm=q_norm[l], k_norm=k_norm[l])
        rnn_w = _rnn_weights(conv_w[l], conv_b[l], lru_wa[l], lru_ba[l], lru_wx[l], lru_bx[l],
                             lru_lambda[l], rnn_out_norm[l])
        bp = yp.shape[0]

        def prompt_attn(q, k, v, kb, vb, kmean):
            return moba_prompt(q, kb, jnp.swapaxes(vb, 1, 2), kmean.reshape(bp, -1, ATTN_WIDTH))

        def prompt_rnn(u, g):
            return rnn_seq(u, g, jnp.zeros((bp, CONV_WIDTH - 1, RNN_WIDTH), F32),
                           jnp.zeros((bp, RNN_WIDTH), F32), rnn_w, SCAN_TILE)

        def sample_attn(q, k, v, kb, vb, kmean, l=l):
            return moba_sample(q, k, v, cache_k[l:l + 1], cache_v[l:l + 1], page_table)

        def sample_rnn(u, g, l=l):
            return rnn_step(u, g, state_conv[l], state_h[l], rnn_w)

        oa_p, or_p, kp, vp, cp, hp = _mixers(yp, prompt_attn, prompt_rnn, lp)
        oa_s, or_s, ks_, vs_, cs, hs = _mixers(ys, sample_attn, sample_rnn, lp)
        x_mid, xn, route = out_proj((oa_p, or_p, yp.reshape(n_p, D_MODEL)), (oa_s, or_s, ys.reshape(n_s, D_MODEL)),
                                    attn_out_norm[l], w_out[l].astype(BF16), norm_ffn[l], router_group_w[l],
                                    router_group_b[l], router_expert_w[l], router_expert_b[l],
                                    min(ROW_TILE, n_p, n_s))
        y_p, y_s = moe_layer(x_mid, xn, route, expert_w_gate[l], expert_w_up[l], expert_w_down[l],
                             ((0, n_p), (n_p, n_s)))
        yp, ys = y_p.reshape(yp.shape), y_s.reshape(ys.shape)
        for lst, val in zip(outs, (kp, vp, cp, hp, ks_, vs_, cs, hs)):
            lst.append(val)
    return (yp, ys) + tuple(jnp.stack(lst) for lst in outs)
```

```python
import functools

import jax
import jax.numpy as jnp
import numpy as np
from jax import lax
from jax.experimental import pallas as pl
from jax.experimental.pallas import tpu as pltpu

F32 = jnp.float32
BF16 = jnp.bfloat16

D_MODEL = 1024
ATTN_WIDTH = 512
RNN_WIDTH = 512
HEAD_DIM = 64
N_HEADS = 8
LRU_BLOCKS = 8
CONV_WIDTH = 4
LRU_C = 8.0
BLOCK = 256
TOP_BLOCKS = 3
N_GROUPS = 4
EXPERTS_PER_GROUP = 8
N_EXPERTS = 32
D_EXPERT = 512
EPS = 1e-6
D_IN = ATTN_WIDTH + 2 * ATTN_WIDTH + 2 * RNN_WIDTH
NEG = -1e30
LOG2E = 1.4426950408889634

VMEM_LIMIT = 56 * 1024 * 1024


def _cparams(sem):
    return pltpu.CompilerParams(dimension_semantics=sem, vmem_limit_bytes=VMEM_LIMIT)


def _split_bf16(a):
    hi = a.astype(BF16)
    lo = (a - hi.astype(F32)).astype(BF16)
    return hi, lo


def _inproj_kernel(x_ref, nw_ref, w_ref, qn_ref, kn_ref, gm_ref,
                   q_ref, k_ref, v_ref, kb_ref, vb_ref, u_ref, g_ref, km_ref):
    x = x_ref[...]
    ms = jnp.mean(x * x, axis=-1, keepdims=True)
    xn = x * lax.rsqrt(ms + EPS) * nw_ref[...]
    z = jnp.dot(xn.astype(BF16), w_ref[...], preferred_element_type=F32)
    aw = ATTN_WIDTH

    def head_norm(t, gain):
        hi, lo = _split_bf16(t * t)
        hm = (jnp.dot(hi, gm_ref[...], preferred_element_type=F32)
              + jnp.dot(lo, gm_ref[...], preferred_element_type=F32))
        return t * lax.rsqrt(hm + EPS) * gain

    q = head_norm(z[:, 0:aw], qn_ref[...])
    k = head_norm(z[:, aw:2 * aw], kn_ref[...])
    v = z[:, 2 * aw:3 * aw]
    q_ref[...] = q
    k_ref[...] = k
    v_ref[...] = v
    kb_ref[...] = k.astype(BF16)
    vb_ref[...] = v.astype(BF16)
    u_ref[...] = z[:, 3 * aw:3 * aw + RNN_WIDTH]
    g_ref[...] = z[:, 3 * aw + RNN_WIDTH:]
    tm = k.shape[0]
    km_ref[0] = jnp.mean(k.reshape(tm // BLOCK, BLOCK, aw), axis=1)


def _head_avg_matrix():
    r = jnp.arange(ATTN_WIDTH) // HEAD_DIM
    return jnp.where(r[:, None] == r[None, :], 1.0 / HEAD_DIM, 0.0).astype(BF16)


def in_proj(x2d, norm_w, w_in_bf, q_norm, k_norm, tm):
    n = x2d.shape[0]
    assert n % tm == 0 and tm % BLOCK == 0
    nt = n // tm
    row = lambda i: (i, 0)
    const = lambda i: (0, 0)
    wspec = lambda shp: pl.BlockSpec(shp, const)
    out_f = jax.ShapeDtypeStruct((n, ATTN_WIDTH), F32)
    out_b = jax.ShapeDtypeStruct((n, ATTN_WIDTH), BF16)
    ospec = pl.BlockSpec((tm, ATTN_WIDTH), row)
    return pl.pallas_call(
        _inproj_kernel,
        grid=(nt,),
        in_specs=[pl.BlockSpec((tm, D_MODEL), row), wspec((1, D_MODEL)), wspec((D_MODEL, D_IN)),
                  wspec((1, ATTN_WIDTH)), wspec((1, ATTN_WIDTH)), wspec((ATTN_WIDTH, ATTN_WIDTH))],
        out_specs=[ospec, ospec, ospec, ospec, ospec, ospec, ospec,
                   pl.BlockSpec((1, tm // BLOCK, ATTN_WIDTH), lambda i: (i, 0, 0))],
        out_shape=[out_f, out_f, out_f, out_b, out_b, out_f, out_f,
                   jax.ShapeDtypeStruct((nt, tm // BLOCK, ATTN_WIDTH), F32)],
        compiler_params=_cparams(("arbitrary",)),
        name="in_proj",
    )(x2d, norm_w.reshape(1, D_MODEL), w_in_bf,
      jnp.tile(q_norm, N_HEADS).reshape(1, ATTN_WIDTH), jnp.tile(k_norm, N_HEADS).reshape(1, ATTN_WIDTH),
      _head_avg_matrix())


def _gelu_tanh(x):
    return 0.5 * x * (1.0 + jnp.tanh(0.7978845608028654 * (x + 0.044715 * (x * x * x))))


def _softplus(z):
    return jnp.maximum(z, 0.0) + jnp.log1p(jnp.exp(-jnp.abs(z)))


def _lru_coeffs(xc, wa_ref, ba_ref, wx_ref, bx_ref, lam_ref):
    xb = xc.astype(BF16)
    r = jax.nn.sigmoid(jnp.dot(xb, wa_ref[...], preferred_element_type=F32) + ba_ref[...])
    i = jax.nn.sigmoid(jnp.dot(xb, wx_ref[...], preferred_element_type=F32) + bx_ref[...])
    log_a = -LRU_C * r * _softplus(-lam_ref[...])
    a = jnp.exp(log_a)
    b = jnp.sqrt(-jnp.tanh(log_a) * (a * a + 1.0)) * i * xc
    return a, b


def _rms_rows(x, gain):
    return x * lax.rsqrt(jnp.mean(x * x, axis=-1, keepdims=True) + EPS) * gain


def _rnn_seq_kernel(u_ref, g_ref, cb_ref, h0_ref, cw_ref, cbias_ref, wa_ref, ba_ref, wx_ref, bx_ref,
                    lam_ref, onw_ref, o_ref, hl_ref, cl_ref, pad_ref, h_ref):
    s = pl.program_id(1)
    ts = u_ref.shape[1]
    w = u_ref.shape[2]

    @pl.when(s == 0)
    def _():
        pad_ref[0:8, :] = cb_ref[0]
        h_ref[...] = h0_ref[0]

    u = u_ref[0]
    pad_ref[8:8 + ts, :] = u
    xc = cbias_ref[...] + cw_ref[3:4, :] * u
    for j in range(CONV_WIDTH - 1):
        xc = xc + cw_ref[j:j + 1, :] * pad_ref[5 + j:5 + j + ts, :]
    tail = pad_ref[ts:ts + 8, :]
    pad_ref[0:8, :] = tail
    cl_ref[0] = tail

    a, b = _lru_coeffs(xc, wa_ref, ba_ref, wx_ref, bx_ref, lam_ref)
    row = lax.broadcasted_iota(jnp.int32, (ts, w), 0)
    d = 1
    while d < ts:
        valid = row >= d
        a_sh = jnp.where(valid, pltpu.roll(a, d, axis=0), 1.0)
        b_sh = jnp.where(valid, pltpu.roll(b, d, axis=0), 0.0)
        b = a * b_sh + b
        a = a * a_sh
        d *= 2
    h = a * h_ref[...] + b
    hl = h[ts - 1:ts, :]
    h_ref[...] = hl
    hl_ref[0] = hl
    o = h * _gelu_tanh(g_ref[0])
    o_ref[0] = _rms_rows(o, onw_ref[...]).astype(o_ref.dtype)


def _dot_nt(a, b):
    return lax.dot_general(a, b, (((1,), (1,)), ((), ())), preferred_element_type=F32)


def _top_blocks_mask(gate, n_valid):
    nblk, tq = gate.shape
    n_iota = lax.broadcasted_iota(jnp.int32, (nblk, tq), 0)
    ok = n_iota < n_valid
    g = jnp.where(ok, gate, -jnp.inf)
    sel = jnp.zeros((nblk, tq), F32)
    for _ in range(TOP_BLOCKS):
        mx = jnp.max(g, axis=0, keepdims=True)
        first = jnp.min(jnp.where(g == mx, n_iota, nblk), axis=0, keepdims=True)
        hit = n_iota == first
        sel = jnp.where(hit, 1.0, sel)
        g = jnp.where(hit, -jnp.inf, g)
    return jnp.where(jnp.logical_and(sel > 0.0, ok), 0.0, NEG)


KEY_GROUP = 4
MAX_BLOCKS = 64
AUG = 128


def _alibi_terms():
    slopes = np.exp2(-8.0 * np.arange(1, N_HEADS + 1) / N_HEADS)
    out = np.zeros((N_HEADS, AUG - MAX_BLOCKS), np.float32)
    for h in range(N_HEADS):
        for j, val in enumerate((slopes[h] * LOG2E, slopes[h] * LOG2E * BLOCK)):
            rest = float(np.float32(val))
            for i in range(3):
                term = float(np.asarray(rest, np.float32).astype(jnp.bfloat16).astype(np.float32))
                out[h, 3 * j + i] = term
                rest -= term
    return jnp.asarray(out.reshape(N_HEADS // 2, 2, 1, AUG - MAX_BLOCKS), dtype=BF16)


def _key_aug_columns(s):
    pos = jnp.arange(s, dtype=jnp.int32)[:, None]
    col = jnp.arange(AUG, dtype=jnp.int32)[None, :]
    blk, off = pos // BLOCK, pos % BLOCK
    vals = jnp.where(col < MAX_BLOCKS, (col == blk).astype(jnp.int32),
                     jnp.where(col < MAX_BLOCKS + 3, off, jnp.where(col < MAX_BLOCKS + 6, blk, 0)))
    return vals.astype(BF16)


def _moba_prompt_kernel(q_ref, ka_ref, vt_ref, km_ref, coef_ref, o_ref, qa_ref, acc_ref, s0_ref, s1_ref):
    qi = pl.program_id(2)
    tq = q_ref.shape[1]
    nblk = km_ref.shape[1]
    t0 = pl.multiple_of(qi * tq, BLOCK)
    q = q_ref[0]
    km_hi, km_lo = _split_bf16(km_ref[0])
    lane = lax.broadcasted_iota(jnp.int32, q.shape, 1)
    key_i = lax.broadcasted_iota(jnp.int32, (BLOCK, tq), 0)
    qry_i = lax.broadcasted_iota(jnp.int32, (BLOCK, tq), 1)
    causal = key_i <= qry_i
    kd = ka_ref[0, 0, pl.ds(t0, BLOCK), :]
    nomask = jnp.zeros((tq, MAX_BLOCKS), BF16)

    init = []
    for e in range(2):
        coef = jnp.broadcast_to(coef_ref[0, e], (tq, AUG - MAX_BLOCKS))
        qh = jnp.where((lane // HEAD_DIM) == e, q, 0.0)
        q_hi, q_lo = _split_bf16(qh)
        gate = _dot_nt(km_hi, q_hi) + _dot_nt(km_hi, q_lo) + _dot_nt(km_lo, q_hi)
        mask_t = _top_blocks_mask(gate, qi)
        mask = jnp.concatenate([mask_t, jnp.zeros((128 - nblk, tq), F32)], axis=0).T[:, :MAX_BLOCKS]
        qs = (qh * (HEAD_DIM ** -0.5 * LOG2E)).astype(BF16)
        qa_ref[e * tq:(e + 1) * tq, :] = jnp.concatenate([qs, mask.astype(BF16), coef], axis=1)
        qd = jnp.concatenate([qs, nomask, coef], axis=1)
        s = jnp.where(causal, _dot_nt(kd, qd), NEG)
        m = jnp.max(s, axis=0, keepdims=True)
        p = jnp.exp2(s - m)
        vt = vt_ref[0, HEAD_DIM * e:HEAD_DIM * (e + 1), pl.ds(t0, BLOCK)]
        acc_ref[e] = jnp.dot(vt, p.astype(BF16), preferred_element_type=F32)
        init += [m, jnp.sum(p, axis=0, keepdims=True)]

    gk = KEY_GROUP * BLOCK

    def scores(g, s_ref):
        base = pl.multiple_of(g * gk, gk)
        s = _dot_nt(ka_ref[0, 0, pl.ds(base, gk), :], qa_ref[...])
        s_ref[...] = s
        return jnp.max(s, axis=0, keepdims=True)

    def softmax_pv(g, s_ref, m_old, l_old, cmax):
        base = pl.multiple_of(g * gk, gk)
        m_new = jnp.maximum(m_old, cmax)
        alpha = jnp.exp2(m_old - m_new)
        p = jnp.exp2(s_ref[...] - m_new)
        pb = p.astype(BF16)
        for e in range(2):
            vt = vt_ref[0, HEAD_DIM * e:HEAD_DIM * (e + 1), pl.ds(base, gk)]
            acc_ref[e] = (alpha[:, e * tq:(e + 1) * tq] * acc_ref[e]
                          + jnp.dot(vt, pb[:, e * tq:(e + 1) * tq], preferred_element_type=F32))
        return m_new, alpha * l_old + jnp.sum(p, axis=0, keepdims=True)

    n_groups = lax.shift_right_logical(qi + (KEY_GROUP - 1), KEY_GROUP.bit_length() - 1)
    n_pairs = lax.shift_right_logical(n_groups + 1, 1)

    def pair(gg, carry, prefetch):
        m_old, l_old, cmax0 = carry
        g0 = 2 * gg
        cmax1 = scores(g0 + 1, s1_ref)
        m_mid, l_mid = softmax_pv(g0, s0_ref, m_old, l_old, cmax0)
        cmax2 = scores(g0 + 2, s0_ref) if prefetch else cmax0
        m_new, l_new = softmax_pv(g0 + 1, s1_ref, m_mid, l_mid, cmax1)
        return m_new, l_new, cmax2

    m0 = jnp.concatenate(init[0::2], axis=1)
    l0 = jnp.concatenate(init[1::2], axis=1)

    @pl.when(qi > 0)
    def _():
        cmax0 = scores(0, s0_ref)
        carry = lax.fori_loop(0, n_pairs - 1, lambda gg, c: pair(gg, c, True), (m0, l0, cmax0))

        def finish(l_fin):
            o_t = jnp.concatenate([acc_ref[e] / l_fin[:, e * tq:(e + 1) * tq] for e in range(2)], axis=0)
            o_ref[0] = o_t.T

        @pl.when(lax.rem(n_groups, 2) == 0)
        def _():
            finish(pair(n_pairs - 1, carry, False)[1])

        @pl.when(lax.rem(n_groups, 2) == 1)
        def _():
            finish(softmax_pv(2 * (n_pairs - 1), s0_ref, *carry)[1])

    @pl.when(qi == 0)
    def _():
        o_t = jnp.concatenate([acc_ref[e] / l0[:, e * tq:(e + 1) * tq] for e in range(2)], axis=0)
        o_ref[0] = o_t.T


def moba_prompt(q, kb, vt, kmean):
    b, s, aw = q.shape
    nblk = s // BLOCK
    hp = aw // 128
    assert nblk <= MAX_BLOCKS and nblk % (2 * KEY_GROUP) == 0 and KEY_GROUP & (KEY_GROUP - 1) == 0
    ka = jnp.concatenate([jnp.swapaxes(kb.reshape(b, s, hp, 128), 1, 2),
                          jnp.broadcast_to(_key_aug_columns(s), (b, hp, s, AUG))], axis=-1)
    return pl.pallas_call(
        _moba_prompt_kernel,
        grid=(b, hp, nblk),
        in_specs=[pl.BlockSpec((1, BLOCK, 128), lambda i, h, j: (i, j, h)),
                  pl.BlockSpec((1, 1, s, 128 + AUG), lambda i, h, j: (i, h, 0, 0)),
                  pl.BlockSpec((1, 128, s), lambda i, h, j: (i, h, 0)),
                  pl.BlockSpec((1, nblk, 128), lambda i, h, j: (i, 0, h)),
                  pl.BlockSpec((1, 2, 1, AUG - MAX_BLOCKS), lambda i, h, j: (h, 0, 0, 0))],
        out_specs=pl.BlockSpec((1, BLOCK, 128), lambda i, h, j: (i, j, h)),
        out_shape=jax.ShapeDtypeStruct((b, s, aw), F32),
        scratch_shapes=[pltpu.VMEM((2 * BLOCK, 128 + AUG), BF16), pltpu.VMEM((2, HEAD_DIM, BLOCK), F32),
                        pltpu.VMEM((KEY_GROUP * BLOCK, 2 * BLOCK), F32),
                        pltpu.VMEM((KEY_GROUP * BLOCK, 2 * BLOCK), F32)],
        compiler_params=_cparams(("arbitrary", "arbitrary", "arbitrary")),
        name="moba_prompt",
    )(q, ka, vt, kmean, _alibi_terms())


ROUTE_LANES = 128


def _route(logits):
    lane = lax.broadcasted_iota(jnp.int32, logits.shape, 1)
    ninf = -jnp.inf
    gl = jnp.where(lane < N_GROUPS, logits, ninf)
    gmax = jnp.max(gl, axis=-1, keepdims=True)
    gsel = jnp.min(jnp.where(gl == gmax, lane, ROUTE_LANES), axis=-1, keepdims=True)
    p_sel = 1.0 / jnp.sum(jnp.exp(gl - gmax), axis=-1, keepdims=True)
    lo = N_GROUPS + EXPERTS_PER_GROUP * gsel
    emask = jnp.logical_and(lane >= lo, lane < lo + EXPERTS_PER_GROUP)
    el = jnp.where(emask, logits, ninf)
    emax = jnp.max(el, axis=-1, keepdims=True)
    pe_un = jnp.exp(el - emax)
    pe = jnp.where(emask, pe_un / jnp.sum(pe_un, axis=-1, keepdims=True), -1.0)
    w1 = jnp.max(pe, axis=-1, keepdims=True)
    l1 = jnp.min(jnp.where(pe == w1, lane, ROUTE_LANES), axis=-1, keepdims=True)
    pe2 = jnp.where(lane == l1, -1.0, pe)
    w2 = jnp.max(pe2, axis=-1, keepdims=True)
    l2 = jnp.min(jnp.where(pe2 == w2, lane, ROUTE_LANES), axis=-1, keepdims=True)
    tot = w1 + w2
    out = jnp.where(lane == 0, (l1 - N_GROUPS).astype(F32), 0.0)
    out = jnp.where(lane == 1, (l2 - N_GROUPS).astype(F32), out)
    out = jnp.where(lane == 2, p_sel * (w1 / tot), out)
    out = jnp.where(lane == 3, p_sel * (w2 / tot), out)
    return out


def _outproj_kernel(oa_ref, or_ref, x_ref, anw_ref, wo_ref, fnw_ref, rwh_ref, rwl_ref, rb_ref,
                    xm_ref, xn_ref, rt_ref):
    a_n = _rms_rows(oa_ref[...], anw_ref[...]).astype(BF16)
    mix = (jnp.dot(a_n, wo_ref[0:ATTN_WIDTH, :], preferred_element_type=F32)
           + jnp.dot(or_ref[...], wo_ref[ATTN_WIDTH:, :], preferred_element_type=F32))
    xm = x_ref[...] + mix
    xm_ref[...] = xm
    xn = _rms_rows(xm, fnw_ref[...])
    xn_ref[...] = xn
    hi, lo = _split_bf16(xn)
    logits = (jnp.dot(hi, rwh_ref[...], preferred_element_type=F32)
              + jnp.dot(lo, rwh_ref[...], preferred_element_type=F32)
              + jnp.dot(hi, rwl_ref[...], preferred_element_type=F32)) + rb_ref[...]
    rt_ref[...] = _route(logits)


def _outproj_streams_kernel(*refs, n_first):
    first, second, rest = refs[0:3], refs[3:6], refs[6:]
    i = pl.program_id(0)

    @pl.when(i < n_first)
    def _():
        _outproj_kernel(*first, *rest)

    @pl.when(i >= n_first)
    def _():
        _outproj_kernel(*second, *rest)


def out_proj(stream_a, stream_b, attn_out_norm, w_out_bf, norm_ffn, rgw, rgb, rew, reb, tm):
    n_a, n_b = stream_a[2].shape[0], stream_b[2].shape[0]
    assert n_a % tm == 0 and n_b % tm == 0
    na, nb = n_a // tm, n_b // tm
    total = n_a + n_b
    rw = jnp.zeros((D_MODEL, ROUTE_LANES), F32).at[:, :N_GROUPS].set(rgw).at[:, N_GROUPS:N_GROUPS + N_EXPERTS].set(rew)
    rb = jnp.zeros((1, ROUTE_LANES), F32).at[0, :N_GROUPS].set(rgb).at[0, N_GROUPS:N_GROUPS + N_EXPERTS].set(reb)
    rwh, rwl = _split_bf16(rw)
    row_a = lambda w: pl.BlockSpec((tm, w), lambda i: (jnp.minimum(i, na - 1), 0))
    row_b = lambda w: pl.BlockSpec((tm, w), lambda i: (jnp.maximum(i - na, 0), 0))
    orow = lambda w: pl.BlockSpec((tm, w), lambda i: (i, 0))
    const = lambda shp: pl.BlockSpec(shp, lambda i: (0, 0))
    return pl.pallas_call(
        functools.partial(_outproj_streams_kernel, n_first=na),
        grid=(na + nb,),
        in_specs=[row_a(ATTN_WIDTH), row_a(RNN_WIDTH), row_a(D_MODEL), row_b(ATTN_WIDTH), row_b(RNN_WIDTH), row_b(D_MODEL),
                  const((1, ATTN_WIDTH)), const((D_MODEL, D_MODEL)), const((1, D_MODEL)), const((D_MODEL, ROUTE_LANES)),
                  const((D_MODEL, ROUTE_LANES)), const((1, ROUTE_LANES))],
        out_specs=[orow(D_MODEL), orow(D_MODEL), orow(ROUTE_LANES)],
        out_shape=[jax.ShapeDtypeStruct((total, D_MODEL), F32), jax.ShapeDtypeStruct((total, D_MODEL), F32),
                   jax.ShapeDtypeStruct((total, ROUTE_LANES), F32)],
        compiler_params=_cparams(("arbitrary",)),
        name="out_proj",
    )(*stream_a, *stream_b, attn_out_norm.reshape(1, ATTN_WIDTH), w_out_bf, norm_ffn.reshape(1, D_MODEL), rwh, rwl, rb)


MOE_ROWS = 256


def _gather_rows(src_hbm, dst, sem, idx_ref, base, count):
    def issue(r, c):
        pltpu.make_async_copy(src_hbm.at[pl.ds(idx_ref[base + r], 1), :], dst.at[pl.ds(r, 1), :], sem).start()
        return c
    lax.fori_loop(0, count, issue, 0, unroll=8)


def _wait_rows(src_hbm, dst, sem, count):
    pltpu.make_async_copy(src_hbm.at[pl.ds(0, count), :], dst, sem).wait()


def _expert_kernel(be_ref, nu_ref, tok_ref, x_hbm, wg_ref, wu_ref, wd_ref, y_ref,
                   xbuf, sems, xb_ref, wg_bf, wu_bf, wd_bf):
    i = pl.program_id(0)
    n_used = nu_ref[0]
    slot = lax.rem(i, 2)

    @pl.when(i == 0)
    def _():
        _gather_rows(x_hbm, xbuf.at[0], sems.at[0], tok_ref, 0, MOE_ROWS)

    changed = jnp.logical_or(i == 0, be_ref[i] != be_ref[jnp.maximum(i - 1, 0)])

    @pl.when(jnp.logical_and(changed, i < n_used))
    def _():
        wg_bf[...] = wg_ref[0].astype(BF16)
        wu_bf[...] = wu_ref[0].astype(BF16)
        wd_bf[...] = wd_ref[0].astype(BF16)

    @pl.when(i < n_used)
    def _():
        _wait_rows(x_hbm, xbuf.at[slot], sems.at[slot], MOE_ROWS)
        xb_ref[...] = xbuf[slot].astype(BF16)

    def compute():
        xb = xb_ref[...]
        hg = jnp.dot(xb, wg_bf[...], preferred_element_type=F32)
        hu = jnp.dot(xb, wu_bf[...], preferred_element_type=F32)
        hdn = (hg * jax.nn.sigmoid(hg) * hu).astype(BF16)
        y_ref[...] = jnp.dot(hdn, wd_bf[...], preferred_element_type=F32)

    @pl.when(i + 1 < n_used)
    def _():
        base = (i + 1) * MOE_ROWS
        for r in range(MOE_ROWS):
            pltpu.make_async_copy(x_hbm.at[pl.ds(tok_ref[base + r], 1), :],
                                  xbuf.at[1 - slot, pl.ds(r, 1), :], sems.at[1 - slot]).start()
        compute()

    @pl.when(i + 1 == n_used)
    def _():
        compute()

    @pl.when(i >= n_used)
    def _():
        y_ref[...] = jnp.zeros_like(y_ref)


def expert_ffn(xn, block_expert, n_used, buf_tok, w_gate, w_up, w_down):
    nb = block_expert.shape[0]
    wmap = lambda i, be, nu, tok: (be[i], 0, 0)
    return pl.pallas_call(
        _expert_kernel,
        grid_spec=pltpu.PrefetchScalarGridSpec(
            num_scalar_prefetch=3,
            grid=(nb,),
            in_specs=[pl.BlockSpec(memory_space=pl.ANY),
                      pl.BlockSpec((1, D_MODEL, D_EXPERT), wmap),
                      pl.BlockSpec((1, D_MODEL, D_EXPERT), wmap),
                      pl.BlockSpec((1, D_EXPERT, D_MODEL), wmap)],
            out_specs=pl.BlockSpec((MOE_ROWS, D_MODEL), lambda i, be, nu, tok: (i, 0)),
            scratch_shapes=[pltpu.VMEM((2, MOE_ROWS, D_MODEL), F32), pltpu.SemaphoreType.DMA((2,)),
                            pltpu.VMEM((MOE_ROWS, D_MODEL), BF16), pltpu.VMEM((D_MODEL, D_EXPERT), BF16), pltpu.VMEM((D_MODEL, D_EXPERT), BF16),
                            pltpu.VMEM((D_EXPERT, D_MODEL), BF16)]),
        out_shape=jax.ShapeDtypeStruct((nb * MOE_ROWS, D_MODEL), F32),
        compiler_params=_cparams(("arbitrary",)),
        name="expert_ffn",
    )(block_expert, n_used, buf_tok, xn, w_gate, w_up, w_down)


COMBINE_ROWS = 256


def _combine_kernel(pos_ref, y_hbm, xm_ref, rt_ref, o_ref, ybuf, sems, *, blk_off):
    i = pl.program_id(0)
    nsteps = pl.num_programs(0)
    slot = lax.rem(i, 2)
    rows = 2 * COMBINE_ROWS

    @pl.when(i == 0)
    def _():
        _gather_rows(y_hbm, ybuf.at[0], sems.at[0], pos_ref, blk_off * rows, rows)

    @pl.when(i + 1 < nsteps)
    def _():
        _gather_rows(y_hbm, ybuf.at[1 - slot], sems.at[1 - slot], pos_ref, (i + 1 + blk_off) * rows, rows)

    _wait_rows(y_hbm, ybuf.at[slot], sems.at[slot], rows)
    rt = rt_ref[...]
    o_ref[...] = (xm_ref[...] + (rt[:, 2:3] * ybuf[slot, 0:COMBINE_ROWS, :]
                                 + rt[:, 3:4] * ybuf[slot, COMBINE_ROWS:rows, :]))


def moe_combine(pos_flat, yb, x_mid, route, row_off, n):
    assert n % COMBINE_ROWS == 0 and row_off % COMBINE_ROWS == 0
    off = row_off // COMBINE_ROWS
    row = lambda w: pl.BlockSpec((COMBINE_ROWS, w), lambda i, pos: (i + off, 0))
    return pl.pallas_call(
        functools.partial(_combine_kernel, blk_off=off),
        grid_spec=pltpu.PrefetchScalarGridSpec(
            num_scalar_prefetch=1,
            grid=(n // COMBINE_ROWS,),
            in_specs=[pl.BlockSpec(memory_space=pl.ANY), row(D_MODEL), row(ROUTE_LANES)],
            out_specs=pl.BlockSpec((COMBINE_ROWS, D_MODEL), lambda i, pos: (i, 0)),
            scratch_shapes=[pltpu.VMEM((2, 2 * COMBINE_ROWS, D_MODEL), F32), pltpu.SemaphoreType.DMA((2,))]),
        out_shape=jax.ShapeDtypeStruct((n, D_MODEL), F32),
        compiler_params=_cparams(("arbitrary",)),
        name="moe_combine",
    )(pos_flat, yb, x_mid, route)


def _moe_plan(route, n):
    a = 2 * n
    flat_e = route[:, 0:2].astype(jnp.int32).reshape(a)
    onehot = (flat_e[:, None] == jnp.arange(N_EXPERTS, dtype=jnp.int32)[None, :]).astype(jnp.int32)
    csum = jnp.cumsum(onehot, axis=0)
    counts = csum[-1]
    rank = jnp.sum(onehot * csum, axis=1) - 1
    padded = (counts + MOE_ROWS - 1) // MOE_ROWS * MOE_ROWS
    pad_end = jnp.cumsum(padded)
    pad_start = pad_end - padded
    start = jnp.cumsum(counts) - counts
    pos = (pad_start[flat_e] + rank).astype(jnp.int32)
    nb = -(-a // MOE_ROWS) + N_EXPERTS
    block_expert = jnp.minimum(jnp.sum((pad_end[None, :] <= (jnp.arange(nb) * MOE_ROWS)[:, None]).astype(jnp.int32),
                                       axis=1), N_EXPERTS - 1).astype(jnp.int32)
    n_used = (pad_end[-1] // MOE_ROWS).astype(jnp.int32).reshape(1)
    order = jnp.argsort(flat_e, stable=True)
    r = jnp.arange(nb * MOE_ROWS, dtype=jnp.int32)
    e_r = jnp.repeat(block_expert, MOE_ROWS)
    j = r - pad_start[e_r]
    buf_tok = jnp.where(j < counts[e_r], order[jnp.clip(start[e_r] + j, 0, a - 1)] // 2, 0).astype(jnp.int32)
    pos_flat = jnp.swapaxes(pos.reshape(n // COMBINE_ROWS, COMBINE_ROWS, 2), 1, 2).reshape(a)
    return block_expert, n_used, buf_tok, pos_flat


def moe_layer(x_mid, xn, route, w_gate, w_up, w_down, splits):
    block_expert, n_used, buf_tok, pos_flat = _moe_plan(route, x_mid.shape[0])
    yb = expert_ffn(xn, block_expert, n_used, buf_tok, w_gate, w_up, w_down)
    return [moe_combine(pos_flat, yb, x_mid, route, off, n) for off, n in splits]


PAGE_CHUNK = 16
LOCAL_ROWS = 8


N_SLOTS = 3


def _moba_sample_kernel(pt_ref, q_ref, kn_ref, vn_ref, ck_hbm, cv_hbm, o_ref,
                        buf, sems, s_ref, p_ref, acc_ref, *, n_pages, page_rows):
    b = pl.program_id(0)
    nb = pl.num_programs(0)
    nh = N_HEADS
    t_len = q_ref.shape[1]
    aw = q_ref.shape[2]
    rows = t_len * nh
    n_chunks = n_pages // PAGE_CHUNK
    ppb = BLOCK // page_rows
    n_blocks = n_pages // ppb
    past = n_pages * page_rows
    per_seq = 2 * n_chunks
    slot0 = lax.rem(b * per_seq, N_SLOTS)

    def slot_of(g):
        return lax.rem(slot0 + g, N_SLOTS)

    def issue(seq, g):
        seq, g = (seq + 1, g - per_seq) if g >= per_seq else (seq, g)
        cache, chunk = (ck_hbm, g) if g < n_chunks else (cv_hbm, g - n_chunks)
        slot = lax.rem(lax.rem(seq * per_seq, N_SLOTS) + g, N_SLOTS)
        for j in range(PAGE_CHUNK):
            pg = pt_ref[seq * n_pages + chunk * PAGE_CHUNK + j]
            pltpu.make_async_copy(cache.at[0, pg], buf.at[slot, j], sems.at[slot]).start()

    def issue_ahead(g):
        if g + 2 < per_seq:
            issue(b, g + 2)
        else:
            @pl.when(b + 1 < nb)
            def _():
                issue(b, g + 2)

    def wait(cache, slot):
        pltpu.make_async_copy(cache.at[0, pl.ds(0, PAGE_CHUNK)], buf.at[slot], sems.at[slot]).wait()

    @pl.when(b == 0)
    def _():
        issue(0, 0)
        issue(0, 1)

    q = q_ref[0]
    r_i = lax.broadcasted_iota(jnp.int32, (rows, aw), 0)
    c_i = lax.broadcasted_iota(jnp.int32, (rows, aw), 1)
    own = (c_i // HEAD_DIM) == lax.rem(r_i, nh)
    qrep = jnp.broadcast_to(q[:, None, :], (t_len, nh, aw)).reshape(rows, aw)
    qs = jnp.where(own, qrep * (HEAD_DIM ** -0.5 * LOG2E), 0.0).astype(BF16)

    for c in range(n_chunks):
        slot = slot_of(c)
        issue_ahead(c)
        wait(ck_hbm, slot)

        def k_page(j, cc, c=c, slot=slot):
            col0 = pl.multiple_of((c * PAGE_CHUNK + j) * page_rows, page_rows)
            kp = buf[slot, j].reshape(aw, page_rows).astype(BF16)
            s_ref[:, pl.ds(col0, page_rows)] = jnp.dot(qs, kp, preferred_element_type=F32)
            return cc
        lax.fori_loop(0, PAGE_CHUNK, k_page, 0, unroll=4)

    lane = lax.broadcasted_iota(jnp.int32, (rows, 128), 1)
    gate = jnp.full((rows, 128), -jnp.inf, F32)
    for n in range(n_blocks):
        gsum = jnp.sum(s_ref[:, n * BLOCK:(n + 1) * BLOCK], axis=-1, keepdims=True)
        gate = jnp.where(lane == n, gsum, gate)
    sel = jnp.zeros((rows, 128), F32)
    for _ in range(min(TOP_BLOCKS, n_blocks)):
        mx = jnp.max(gate, axis=-1, keepdims=True)
        first = jnp.min(jnp.where(gate == mx, lane, 128), axis=-1, keepdims=True)
        hit = lane == first
        sel = jnp.where(hit, 1.0, sel)
        gate = jnp.where(hit, -jnp.inf, gate)

    r_col = lax.broadcasted_iota(jnp.int32, (rows, 1), 0)
    head_r = lax.rem(r_col, nh)
    tok_r = (r_col // nh).astype(F32)
    slope2 = lax.bitcast_convert_type(lax.shift_left(126 - head_r, 23), F32) * LOG2E
    key_in_blk = lax.broadcasted_iota(jnp.int32, (rows, BLOCK), 1).astype(F32)
    base_bias = slope2 * key_in_blk

    def biased(n):
        chosen = sel[:, n:n + 1] > 0.0
        blk = jnp.where(chosen, slope2 * (float(n * BLOCK - past) - tok_r), NEG)
        return s_ref[:, n * BLOCK:(n + 1) * BLOCK] + (base_bias + blk)

    s_loc = _dot_nt(qs, kn_ref[0].astype(BF16))
    l_col = lax.broadcasted_iota(jnp.int32, s_loc.shape, 1)
    l_tok = lax.broadcasted_iota(jnp.int32, s_loc.shape, 0) // nh
    s_loc = jnp.where(jnp.logical_and(l_col <= l_tok, l_col < t_len),
                      s_loc + slope2 * (l_col.astype(F32) - tok_r), NEG)

    m_acc = biased(0)
    for n in range(1, n_blocks):
        m_acc = jnp.maximum(m_acc, biased(n))
    m = jnp.maximum(jnp.max(m_acc, axis=-1, keepdims=True), jnp.max(s_loc, axis=-1, keepdims=True))
    p_loc = jnp.exp2(s_loc - m)
    l_acc = jnp.zeros((rows, BLOCK), F32)
    for n in range(n_blocks):
        pb = jnp.exp2(biased(n) - m)
        l_acc = l_acc + pb
        p_ref[:, n * BLOCK:(n + 1) * BLOCK] = pb.astype(BF16)
    l = jnp.sum(l_acc, axis=-1, keepdims=True) + jnp.sum(p_loc, axis=-1, keepdims=True)
    acc_ref[...] = jnp.dot(p_loc.astype(BF16), vn_ref[0].astype(BF16), preferred_element_type=F32)

    for c in range(n_chunks):
        slot = slot_of(n_chunks + c)
        issue_ahead(n_chunks + c)
        wait(cv_hbm, slot)

        def v_page(j, cc, c=c, slot=slot):
            col0 = pl.multiple_of((c * PAGE_CHUNK + j) * page_rows, page_rows)
            vp = buf[slot, j].reshape(aw, page_rows).astype(BF16)
            acc_ref[...] += _dot_nt(p_ref[:, pl.ds(col0, page_rows)], vp)
            return cc
        lax.fori_loop(0, PAGE_CHUNK, v_page, 0, unroll=4)

    o_full = jnp.where(own, acc_ref[...] / l, 0.0)
    o_ref[0] = jnp.sum(o_full.reshape(t_len, nh, aw), axis=1)


def moba_sample(q, k, v, cache_k, cache_v, page_table):
    db, t_len, aw = q.shape
    n_pages = page_table.shape[1]
    page_rows = cache_k.shape[2]
    assert n_pages % PAGE_CHUNK == 0 and BLOCK % page_rows == 0 and page_rows % 128 == 0
    assert t_len <= LOCAL_ROWS and n_pages * page_rows // BLOCK <= 128
    rows = t_len * N_HEADS
    n_keys = n_pages * page_rows
    pages = lambda c: jnp.transpose(c, (0, 1, 3, 4, 2))
    padl = lambda a: jnp.pad(a, ((0, 0), (0, LOCAL_ROWS - t_len), (0, 0)))
    per_seq = lambda r: pl.BlockSpec((1, r, aw), lambda i, pt: (i, 0, 0))
    kern = functools.partial(_moba_sample_kernel, n_pages=n_pages, page_rows=page_rows)
    return pl.pallas_call(
        kern,
        grid_spec=pltpu.PrefetchScalarGridSpec(
            num_scalar_prefetch=1,
            grid=(db,),
            in_specs=[per_seq(t_len), per_seq(LOCAL_ROWS), per_seq(LOCAL_ROWS),
                      pl.BlockSpec(memory_space=pl.ANY), pl.BlockSpec(memory_space=pl.ANY)],
            out_specs=per_seq(t_len),
            scratch_shapes=[pltpu.VMEM((N_SLOTS, PAGE_CHUNK, N_HEADS, HEAD_DIM, page_rows), F32),
                            pltpu.SemaphoreType.DMA((N_SLOTS,)),
                            pltpu.VMEM((rows, n_keys), F32),
                            pltpu.VMEM((rows, n_keys), BF16),
                            pltpu.VMEM((rows, aw), F32)]),
        out_shape=jax.ShapeDtypeStruct((db, t_len, aw), F32),
        compiler_params=_cparams(("arbitrary",)),
        name="moba_sample",
    )(page_table.reshape(-1), q, padl(k), padl(v), pages(cache_k), pages(cache_v))


def _blockdiag(w):
    nb, c, d = w.shape
    eye = jnp.eye(nb, dtype=w.dtype)
    return (eye[:, None, :, None] * w[:, :, None, :]).reshape(nb * c, nb * d)


def _rnn_weights(conv_w, conv_b, lru_wa, lru_ba, lru_wx, lru_bx, lru_lambda, rnn_out_norm):
    r1 = lambda a: a.reshape(1, RNN_WIDTH)
    return (conv_w, r1(conv_b), _blockdiag(lru_wa).astype(BF16), r1(lru_ba),
            _blockdiag(lru_wx).astype(BF16), r1(lru_bx), r1(lru_lambda), r1(rnn_out_norm))


def rnn_seq(u, g, conv_buf, h0, weights, ts):
    b, s, w = u.shape
    assert s % ts == 0 and ts % 8 == 0
    cb8 = jnp.concatenate([jnp.zeros((b, 8 - (CONV_WIDTH - 1), w), F32), conv_buf], axis=1)
    tile = pl.BlockSpec((1, ts, w), lambda i, j: (i, j, 0))
    perb = lambda r: pl.BlockSpec((1, r, w), lambda i, j: (i, 0, 0))
    const = lambda shp: pl.BlockSpec(shp, lambda i, j: (0, 0))
    wspecs = [const((CONV_WIDTH, w)), const((1, w)), const((w, w)), const((1, w)), const((w, w)),
              const((1, w)), const((1, w)), const((1, w))]
    o, hl, cl = pl.pallas_call(
        _rnn_seq_kernel,
        grid=(b, s // ts),
        in_specs=[tile, tile, perb(8), perb(1)] + wspecs,
        out_specs=[tile, perb(1), perb(8)],
        out_shape=[jax.ShapeDtypeStruct((b, s, w), BF16), jax.ShapeDtypeStruct((b, 1, w), F32),
                   jax.ShapeDtypeStruct((b, 8, w), F32)],
        scratch_shapes=[pltpu.VMEM((ts + 8, w), F32), pltpu.VMEM((1, w), F32)],
        compiler_params=_cparams(("arbitrary", "arbitrary")),
        name="rnn_seq",
    )(u, g, cb8, h0.reshape(b, 1, w), *weights)
    return o, hl.reshape(b, w), cl[:, 8 - (CONV_WIDTH - 1):, :]


def _rnn_step_kernel(u_ref, g_ref, cb_ref, h0_ref, cw_ref, cbias_ref, wa_ref, ba_ref, wx_ref, bx_ref,
                     lam_ref, onw_ref, o_ref, hl_ref, cl_ref):
    t_len = u_ref.shape[0]
    taps = [cb_ref[j] for j in range(CONV_WIDTH - 1)] + [u_ref[t] for t in range(t_len)]
    h = h0_ref[...]
    for t in range(t_len):
        xc = cbias_ref[...]
        for j in range(CONV_WIDTH):
            xc = xc + cw_ref[j:j + 1, :] * taps[t + j]
        a, b = _lru_coeffs(xc, wa_ref, ba_ref, wx_ref, bx_ref, lam_ref)
        h = a * h + b
        o_ref[t] = _rms_rows(h * _gelu_tanh(g_ref[t]), onw_ref[...]).astype(o_ref.dtype)
    hl_ref[...] = h
    for j in range(CONV_WIDTH - 1):
        cl_ref[j] = taps[t_len + j]


def rnn_step(u, g, conv_buf, h0, weights):
    db, t_len, w = u.shape
    tm = lambda a: jnp.swapaxes(a, 0, 1)
    o, hl, cl = pl.pallas_call(
        _rnn_step_kernel,
        out_shape=[jax.ShapeDtypeStruct((t_len, db, w), BF16), jax.ShapeDtypeStruct((db, w), F32),
                   jax.ShapeDtypeStruct((CONV_WIDTH - 1, db, w), F32)],
        compiler_params=pltpu.CompilerParams(vmem_limit_bytes=VMEM_LIMIT),
        name="rnn_step",
    )(tm(u), tm(g), tm(conv_buf), h0, *weights)
    return tm(o), hl, tm(cl)


ROW_TILE = 512
SCAN_TILE = 256


def _mixers(x, attn_fn, rnn_fn, lp):
    b, t, d = x.shape
    n = b * t
    q, k, v, kb, vb, u, g, kmean = in_proj(x.reshape(n, d), lp["norm_mix"], lp["w_in"], lp["q_norm"], lp["k_norm"],
                                           min(ROW_TILE, n))
    sh = lambda a: a.reshape(b, t, -1)
    o_attn = attn_fn(sh(q), sh(k), sh(v), sh(kb), sh(vb), kmean)
    o_rnn, h_last, conv_last = rnn_fn(sh(u), sh(g))
    heads = lambda a: a.reshape(b, t, N_HEADS, HEAD_DIM)
    return o_attn.reshape(n, -1), o_rnn.reshape(n, -1), heads(k), heads(v), conv_last, h_last


def kernel(x_prompt, x_sample, cache_k, cache_v, state_conv, state_h, page_table, norm_mix, w_in, q_norm, k_norm, conv_w, conv_b, lru_wa, lru_ba, lru_wx, lru_bx, lru_lambda, attn_out_norm, rnn_out_norm, w_out, norm_ffn, router_group_w, router_group_b, router_expert_w, router_expert_b, expert_w_gate, expert_w_up, expert_w_down):
    depth = w_in.shape[0]
    yp, ys = x_prompt, x_sample
    n_p = yp.shape[0] * yp.shape[1]
    n_s = ys.shape[0] * ys.shape[1]
    outs = [[] for _ in range(8)]
    for l in range(depth):
        lp = dict(norm_mix=norm_mix[l], w_in=w_in[l].astype(BF16), q_norm=q_norm[l], k_norm=k_norm[l])
        rnn_w = _rnn_weights(conv_w[l], conv_b[l], lru_wa[l], lru_ba[l], lru_wx[l], lru_bx[l],
                             lru_lambda[l], rnn_out_norm[l])
        bp = yp.shape[0]

        def prompt_attn(q, k, v, kb, vb, kmean):
            return moba_prompt(q, kb, jnp.swapaxes(vb, 1, 2), kmean.reshape(bp, -1, ATTN_WIDTH))

        def prompt_rnn(u, g):
            return rnn_seq(u, g, jnp.zeros((bp, CONV_WIDTH - 1, RNN_WIDTH), F32),
                           jnp.zeros((bp, RNN_WIDTH), F32), rnn_w, SCAN_TILE)

        def sample_attn(q, k, v, kb, vb, kmean, l=l):
            return moba_sample(q, k, v, cache_k[l:l + 1], cache_v[l:l + 1], page_table)

        def sample_rnn(u, g, l=l):
            return rnn_step(u, g, state_conv[l], state_h[l], rnn_w)

        oa_p, or_p, kp, vp, cp, hp = _mixers(yp, prompt_attn, prompt_rnn, lp)
        oa_s, or_s, ks_, vs_, cs, hs = _mixers(ys, sample_attn, sample_rnn, lp)
        x_mid, xn, route = out_proj((oa_p, or_p, yp.reshape(n_p, D_MODEL)), (oa_s, or_s, ys.reshape(n_s, D_MODEL)),
                                    attn_out_norm[l], w_out[l].astype(BF16), norm_ffn[l], router_group_w[l],
                                    router_group_b[l], router_expert_w[l], router_expert_b[l],
                                    min(ROW_TILE, n_p, n_s))
        y_p, y_s = moe_layer(x_mid, xn, route, expert_w_gate[l], expert_w_up[l], expert_w_down[l],
                             ((0, n_p), (n_p, n_s)))
        yp, ys = y_p.reshape(yp.shape), y_s.reshape(ys.shape)
        for lst, val in zip(outs, (kp, vp, cp, hp, ks_, vs_, cs, hs)):
            lst.append(val)
    return (yp, ys) + tuple(jnp.stack(lst) for lst in outs)
```
